```python
import math
import jax, jax.numpy as jnp
from jax import lax
import numpy as np

D_MODEL = 1024
BATCH = 2
SEQ = 16384
DEPTH = 2

N_MEM = 256
DN_HEADS = 4
DN_DK = 128
DN_DV = 128
DN_CONV = 4
DN_CHUNK = 64
SWA_HEADS = 8
SWA_KV_HEADS = 2
SWA_DH = 64
WINDOW = 128
XA_HEADS = 4
XA_DH = 128
D_FF = 2816
N_BRANCH = 3
BRANCH_W = 512
DEEPNORM_ALPHA = (2 * DEPTH) ** 0.25
DEEPNORM_BETA = (8 * DEPTH) ** -0.25
LN_EPS = 1e-5
RMS_EPS = 1e-6
NEG_INF = -1e30
IN_SPLITS = (DN_HEADS * DN_DK, DN_HEADS * DN_DK, DN_HEADS * DN_DV, DN_HEADS, DN_HEADS,
             DN_HEADS * DN_DV, SWA_HEADS * SWA_DH, SWA_KV_HEADS * SWA_DH, SWA_KV_HEADS * SWA_DH,
             XA_HEADS * XA_DH, N_BRANCH * D_MODEL)
D_IN = sum(IN_SPLITS)
VALUE_SEGMENTS = (2, 8)

kernel_name = "hybrid_deltanet_swa_sink_memxattn_macaron_deepnorm"


def layer_norm(x, g, b):
    xf = x.astype(jnp.float32)
    mu = xf.mean(-1, keepdims=True)
    var = jnp.square(xf - mu).mean(-1, keepdims=True)
    return ((xf - mu) * lax.rsqrt(var + LN_EPS) * g.astype(jnp.float32) + b.astype(jnp.float32)).astype(x.dtype)


def swiglu(x, w_gu, w_down):
    gate, up = jnp.split(x @ w_gu, 2, axis=-1)
    return (jax.nn.silu(gate) * up) @ w_down


def causal_depthwise_conv(x, w):
    K, C = w.shape
    return lax.conv_general_dilated(x, w[:, None, :].astype(x.dtype), window_strides=(1,),
                                    padding=[(K - 1, 0)], dimension_numbers=('NWC', 'WIO', 'NWC'),
                                    feature_group_count=C)


def gated_delta_rule(q, k, v, g, beta):
    f32 = jnp.float32
    B_, S_, H, dk = q.shape
    dv = v.shape[-1]
    C = DN_CHUNK
    N = S_ // C

    def chunks(t):
        t = t.astype(f32).reshape((B_, N, C, H) + t.shape[3:])
        return jnp.moveaxis(t, 3, 1)

    q = chunks(q) * (dk ** -0.5)
    k, v, beta, g = chunks(k), chunks(v), chunks(beta), chunks(g)
    g = jnp.cumsum(g, axis=-1)
    tril = jnp.tril(jnp.ones((C, C), bool))
    strict = jnp.tril(jnp.ones((C, C), bool), -1)
    decay = jnp.exp(jnp.where(tril, g[..., :, None] - g[..., None, :], NEG_INF))

    k_beta = k * beta[..., None]
    a = jnp.where(strict, jnp.einsum('bhncd,bhnsd->bhncs', k_beta, k) * decay, 0.0)
    t_mat = a + jnp.eye(C, dtype=f32)
    rhs = jnp.concatenate([v * beta[..., None], k_beta * jnp.exp(g)[..., None]], axis=-1)
    sol = lax.linalg.triangular_solve(t_mat, rhs, left_side=True, lower=True, unit_diagonal=True)
    u, w = sol[..., :dv], sol[..., dv:]

    qk = jnp.where(tril, jnp.einsum('bhncd,bhnsd->bhncs', q, k) * decay, 0.0)
    g_last = g[..., -1]
    k_tail = k * jnp.exp(g_last[..., None] - g)[..., None]
    q_dec = q * jnp.exp(g)[..., None]

    def step(S, xs):
        q_i, qk_i, u_i, w_i, kt_i, gl_i = xs
        v_new = u_i - jnp.einsum('bhcd,bhde->bhce', w_i, S)
        o = jnp.einsum('bhcd,bhde->bhce', q_i, S) + jnp.einsum('bhcs,bhse->bhce', qk_i, v_new)
        S = S * jnp.exp(gl_i)[..., None, None] + jnp.einsum('bhcd,bhce->bhde', kt_i, v_new)
        return S, o

    xs = tuple(jnp.moveaxis(t, 2, 0) for t in (q_dec, qk, u, w, k_tail, g_last))
    S0 = jnp.zeros((B_, H, dk, dv), f32)
    _, o = lax.scan(step, S0, xs)
    o = jnp.moveaxis(o, 0, 2)
    return jnp.moveaxis(o, 1, 3).reshape(B_, S_, H, dv)


def deltanet_branch(q, k, v, b, a, z, conv_w, a_log, dt_bias, norm_w):
    f32 = jnp.float32
    B_, S_, _ = q.shape
    qkv = jax.nn.silu(causal_depthwise_conv(jnp.concatenate([q, k, v], axis=-1), conv_w))
    q, k, v = jnp.split(qkv.astype(f32), 3, axis=-1)
    q = q.reshape(B_, S_, DN_HEADS, DN_DK)
    k = k.reshape(B_, S_, DN_HEADS, DN_DK)
    v = v.reshape(B_, S_, DN_HEADS, DN_DV)
    q = q * lax.rsqrt(jnp.sum(q * q, -1, keepdims=True) + RMS_EPS)
    k = k * lax.rsqrt(jnp.sum(k * k, -1, keepdims=True) + RMS_EPS)
    beta = jax.nn.sigmoid(b.astype(f32))
    g = -jnp.exp(a_log.astype(f32)) * jax.nn.softplus(a.astype(f32) + dt_bias.astype(f32))
    o = gated_delta_rule(q, k, v, g, beta)
    o = o * lax.rsqrt(jnp.mean(o * o, -1, keepdims=True) + RMS_EPS) * norm_w.astype(f32)
    o = o * jax.nn.silu(z.astype(f32).reshape(B_, S_, DN_HEADS, DN_DV))
    return o.reshape(B_, S_, DN_HEADS * DN_DV).astype(z.dtype)


def sliding_window_attention(q, k, v, sinks):
    f32 = jnp.float32
    B_, S_, _ = q.shape
    Hkv, G, dh, W = SWA_KV_HEADS, SWA_HEADS // SWA_KV_HEADS, SWA_DH, WINDOW
    nb = S_ // W
    qb = q.reshape(B_, nb, W, Hkv, G, dh)

    def band(t):
        tb = t.reshape(B_, nb, W, Hkv, dh)
        prev = jnp.pad(tb[:, :-1], ((0, 0), (1, 0), (0, 0), (0, 0), (0, 0)))
        return jnp.concatenate([prev, tb], axis=2)

    kb, vb = band(k), band(v)
    s = jnp.einsum('bnqhgd,bnkhd->bnhgqk', qb, kb).astype(f32) * (dh ** -0.5)
    q_pos = jnp.arange(nb)[:, None] * W + jnp.arange(W)[None, :]
    k_pos = jnp.arange(nb)[:, None] * W - W + jnp.arange(2 * W)[None, :]
    diff = q_pos[:, :, None] - k_pos[:, None, :]
    mask = (diff >= 0) & (diff < W) & (k_pos[:, None, :] >= 0)
    s = jnp.where(mask[None, :, None, None], s, NEG_INF)
    sink = sinks.astype(f32).reshape(Hkv, G)[None, None, :, :, None, None]
    m = jnp.maximum(s.max(-1, keepdims=True), sink)
    p = jnp.exp(s - m)
    p = p / (p.sum(-1, keepdims=True) + jnp.exp(sink - m))
    o = jnp.einsum('bnhgqk,bnkhd->bnqhgd', p.astype(v.dtype), vb)
    return o.reshape(B_, S_, SWA_HEADS * dh)


def memory_cross_attention(q, mem_n, w_mem_kv):
    B_, S_, _ = q.shape
    M = mem_n.shape[1]
    k, v = jnp.split(mem_n @ w_mem_kv, 2, axis=-1)
    k = k.reshape(B_, M, XA_HEADS, XA_DH)
    v = v.reshape(B_, M, XA_HEADS, XA_DH)
    q = q.reshape(B_, S_, XA_HEADS, XA_DH)
    s = jnp.einsum('bshd,bmhd->bhsm', q, k).astype(jnp.float32) * (XA_DH ** -0.5)
    p = jax.nn.softmax(s, axis=-1)
    o = jnp.einsum('bhsm,bmhd->bshd', p.astype(v.dtype), v)
    return o.reshape(B_, S_, XA_HEADS * XA_DH)


def hybrid_layer(x, mem_n, ln_g, ln_b, ffn1_w_gu, ffn1_w_down, w_in, dn_conv_w, dn_a_log,
                 dn_dt_bias, dn_norm_w, swa_sinks, w_mem_kv, w_branch, w_out, ffn2_w_gu, ffn2_w_down):
    B_, S_, D = x.shape
    h = layer_norm(DEEPNORM_ALPHA * x + 0.5 * swiglu(x, ffn1_w_gu, ffn1_w_down), ln_g[0], ln_b[0])
    split_idx = [int(i) for i in np.cumsum(IN_SPLITS)[:-1]]
    (dn_q, dn_k, dn_v, dn_b, dn_a, dn_z, sw_q, sw_k, sw_v, xa_q, gates) = jnp.split(h @ w_in, split_idx, axis=-1)
    o_dn = deltanet_branch(dn_q, dn_k, dn_v, dn_b, dn_a, dn_z, dn_conv_w, dn_a_log, dn_dt_bias, dn_norm_w)
    o_sw = sliding_window_attention(sw_q, sw_k, sw_v, swa_sinks)
    o_xa = memory_cross_attention(xa_q, mem_n, w_mem_kv)
    branches = jnp.stack([o_dn, o_sw, o_xa], axis=2)
    gates = jax.nn.sigmoid(gates.reshape(B_, S_, N_BRANCH, D))
    merged = jnp.sum(gates * jnp.einsum('bsnc,ncd->bsnd', branches, w_branch), axis=2)
    h = layer_norm(DEEPNORM_ALPHA * h + merged @ w_out, ln_g[1], ln_b[1])
    return layer_norm(DEEPNORM_ALPHA * h + 0.5 * swiglu(h, ffn2_w_gu, ffn2_w_down), ln_g[2], ln_b[2])


def setup_inputs(seed: int = 0) -> dict:
    key = jax.random.key(seed)
    ks = jax.random.split(key, 20)
    f32 = jnp.float32
    nrm = lambda k, shape, s: jax.random.normal(k, shape, f32) * s
    col_scale = np.concatenate([np.full(n, DEEPNORM_BETA if i in VALUE_SEGMENTS else 1.0, np.float32)
                                for i, n in enumerate(IN_SPLITS)])
    kv_scale = np.concatenate([np.ones(XA_HEADS * XA_DH, np.float32),
                               np.full(XA_HEADS * XA_DH, DEEPNORM_BETA, np.float32)])
    dt = jnp.exp(jax.random.uniform(ks[9], (DEPTH, DN_HEADS), f32, math.log(1e-3), math.log(1e-1)))
    return {
        "x": nrm(ks[0], (BATCH, SEQ, D_MODEL), 1.0),
        "mem": nrm(ks[1], (BATCH, N_MEM, D_MODEL), 1.0),
        "mem_ln_g": 1.0 + nrm(ks[2], (D_MODEL,), 0.02),
        "mem_ln_b": nrm(ks[3], (D_MODEL,), 0.02),
        "ln_g": 1.0 + nrm(ks[4], (DEPTH, 3, D_MODEL), 0.02),
        "ln_b": nrm(ks[5], (DEPTH, 3, D_MODEL), 0.02),
        "ffn1_w_gu": nrm(ks[6], (DEPTH, D_MODEL, 2 * D_FF), DEEPNORM_BETA * D_MODEL ** -0.5),
        "ffn1_w_down": nrm(ks[7], (DEPTH, D_FF, D_MODEL), DEEPNORM_BETA * D_FF ** -0.5),
        "w_in": nrm(ks[8], (DEPTH, D_MODEL, D_IN), D_MODEL ** -0.5) * jnp.asarray(col_scale),
        "dn_conv_w": nrm(ks[10], (DEPTH, DN_CONV, 3 * DN_HEADS * DN_DK), DN_CONV ** -0.5),
        "dn_a_log": jnp.log(jax.random.uniform(ks[11], (DEPTH, DN_HEADS), f32, 1.0, 16.0)),
        "dn_dt_bias": dt + jnp.log(-jnp.expm1(-dt)),
        "dn_norm_w": 1.0 + nrm(ks[12], (DEPTH, DN_DV), 0.02),
        "swa_sinks": nrm(ks[13], (DEPTH, SWA_HEADS), 0.5),
        "w_mem_kv": nrm(ks[14], (DEPTH, D_MODEL, 2 * XA_HEADS * XA_DH), D_MODEL ** -0.5) * jnp.asarray(kv_scale),
        "w_branch": nrm(ks[15], (DEPTH, N_BRANCH, BRANCH_W, D_MODEL), DEEPNORM_BETA * BRANCH_W ** -0.5),
        "w_out": nrm(ks[16], (DEPTH, D_MODEL, D_MODEL), DEEPNORM_BETA * D_MODEL ** -0.5),
        "ffn2_w_gu": nrm(ks[17], (DEPTH, D_MODEL, 2 * D_FF), DEEPNORM_BETA * D_MODEL ** -0.5),
        "ffn2_w_down": nrm(ks[18], (DEPTH, D_FF, D_MODEL), DEEPNORM_BETA * D_FF ** -0.5),
    }


def reference(x, mem, mem_ln_g, mem_ln_b, ln_g, ln_b, ffn1_w_gu, ffn1_w_down, w_in, dn_conv_w,
              dn_a_log, dn_dt_bias, dn_norm_w, swa_sinks, w_mem_kv, w_branch, w_out, ffn2_w_gu, ffn2_w_down):
    mem_n = layer_norm(mem, mem_ln_g, mem_ln_b)
    for l in range(DEPTH):
        x = hybrid_layer(x, mem_n, ln_g[l], ln_b[l], ffn1_w_gu[l], ffn1_w_down[l], w_in[l], dn_conv_w[l],
                         dn_a_log[l], dn_dt_bias[l], dn_norm_w[l], swa_sinks[l], w_mem_kv[l], w_branch[l],
                         w_out[l], ffn2_w_gu[l], ffn2_w_down[l])
    return x
```

```python
import functools

import jax
import jax.numpy as jnp
from jax import lax
from jax.experimental import pallas as pl
from jax.experimental.pallas import tpu as pltpu

F32 = jnp.float32
BF16 = jnp.bfloat16
HIGHEST = lax.Precision.HIGHEST

DN_HEADS, DN_DK, DN_DV, DN_CONV, DN_CHUNK = 4, 128, 128, 4, 64
SWA_HEADS, SWA_KV_HEADS, SWA_DH, WINDOW = 8, 2, 64, 128
XA_HEADS, XA_DH = 4, 128
N_BRANCH, BRANCH_W = 3, 512
LN_EPS, RMS_EPS, NEG_INF = 1e-5, 1e-6, -1e30

LANES = 128
SUBLANES = 8
VMEM_LIMIT_BYTES = 56 * 1024 * 1024

TOKEN_TILE = 512
FF_CHUNK = 256
MERGE_CHUNK = 256


def _dot(a, b):
    return jnp.dot(a, b, preferred_element_type=F32)


def _dot_nt(a, b, precision=None):
    return lax.dot_general(a, b, (((1,), (1,)), ((), ())), precision=precision,
                           preferred_element_type=F32)


def _dot_tn(a, b):
    return lax.dot_general(a, b, (((0,), (0,)), ((), ())), preferred_element_type=F32)


def _layer_norm(y, g, b):
    mu = jnp.mean(y, axis=-1, keepdims=True)
    d = y - mu
    var = jnp.mean(d * d, axis=-1, keepdims=True)
    return d * lax.rsqrt(var + LN_EPS) * g + b


def _silu(x):
    return x * jax.nn.sigmoid(x)


def _ffn_ln_kernel(x_ref, wg_ref, wu_ref, wd_ref, g_ref, b_ref, o_ref, acc_ref, *, alpha, n_chunks):
    xf = x_ref[...]
    xb = xf.astype(BF16)
    for c in range(n_chunks):
        gate = _dot(xb, wg_ref[c])
        up = _dot(xb, wu_ref[c])
        act = (_silu(gate) * up).astype(BF16)
        part = _dot(act, wd_ref[c])
        if c == 0:
            acc_ref[...] = part
        else:
            acc_ref[...] += part
    y = alpha * xf + 0.5 * acc_ref[...]
    o_ref[...] = _layer_norm(y, g_ref[...], b_ref[...])


def _ffn_ln(x2d, wg3, wu3, wd3, g, b, alpha):
    t, d = x2d.shape
    n_chunks, _, fc = wg3.shape
    tm = min(TOKEN_TILE, t)
    const3 = lambda i: (0, 0, 0)
    return pl.pallas_call(
        functools.partial(_ffn_ln_kernel, alpha=alpha, n_chunks=n_chunks),
        out_shape=jax.ShapeDtypeStruct((t, d), F32),
        grid=(t // tm,),
        in_specs=[
            pl.BlockSpec((tm, d), lambda i: (i, 0)),
            pl.BlockSpec((n_chunks, d, fc), const3),
            pl.BlockSpec((n_chunks, d, fc), const3),
            pl.BlockSpec((n_chunks, fc, d), const3),
            pl.BlockSpec((1, d), lambda i: (0, 0)),
            pl.BlockSpec((1, d), lambda i: (0, 0)),
        ],
        out_specs=pl.BlockSpec((tm, d), lambda i: (i, 0)),
        scratch_shapes=[pltpu.VMEM((tm, d), F32)],
        compiler_params=pltpu.CompilerParams(
            dimension_semantics=("arbitrary",), vmem_limit_bytes=VMEM_LIMIT_BYTES),
        name="ffn_ln",
    )(x2d, wg3, wu3, wd3, g, b)


def _unit_lower_inverse(a_strict):
    c = a_strict.shape[0]
    row = lax.broadcasted_iota(jnp.int32, (c, c), 0)
    col = lax.broadcasted_iota(jnp.int32, (c, c), 1)
    eye = (row == col).astype(F32)
    x = -a_strict
    p = eye + x
    y = jnp.dot(x, x, precision=HIGHEST, preferred_element_type=F32)
    power = 2
    while power < c:
        if 2 * power < c:
            stacked = jnp.dot(jnp.concatenate([p, y], axis=0), y, precision=HIGHEST,
                              preferred_element_type=F32)
            p = p + stacked[:c]
            y = stacked[c:]
        else:
            p = p + jnp.dot(p, y, precision=HIGHEST, preferred_element_type=F32)
        power *= 2
    return p


def _mixers_kernel(h_ref, wqkv_ref, wba_ref, wz_ref, wswq_ref, wswkv_ref, convw_ref, alog_ref,
                   dtb_ref, normw_ref, sinks_ref, odn_ref, osw_ref,
                   xbuf, q_s, k_s, v_s, z_s, g_s, beta_s, state, kvbuf, *, tm):
    j = pl.program_id(1)
    n_qkv = 3 * DN_HEADS * DN_DK
    halo = SUBLANES

    @pl.when(j == 0)
    def _reset():
        xbuf[0:halo, :] = jnp.zeros((halo, n_qkv), F32)
        state[...] = jnp.zeros_like(state)
        kvbuf[0:WINDOW, :] = jnp.zeros((WINDOW, kvbuf.shape[1]), BF16)

    hb = h_ref[0].astype(BF16)

    pre = _dot(hb, wqkv_ref[...])
    xbuf[halo:halo + tm, :] = pre
    cw = convw_ref[...]
    conv = cw[DN_CONV - 1:DN_CONV, :] * pre
    for s in range(1, DN_CONV):
        conv = conv + cw[DN_CONV - 1 - s:DN_CONV - s, :] * xbuf[halo - s:halo - s + tm, :]
    xbuf[0:halo, :] = xbuf[tm:tm + halo, :]
    qkv = _silu(conv)
    for hd in range(DN_HEADS):
        lo = hd * DN_DK
        qh = qkv[:, lo:lo + DN_DK]
        q_s[:, lo:lo + DN_DK] = qh * (lax.rsqrt(jnp.sum(qh * qh, -1, keepdims=True) + RMS_EPS)
                                      * (DN_DK ** -0.5))
        kh = qkv[:, DN_HEADS * DN_DK + lo:DN_HEADS * DN_DK + lo + DN_DK]
        k_s[:, lo:lo + DN_DK] = kh * lax.rsqrt(jnp.sum(kh * kh, -1, keepdims=True) + RMS_EPS)
    v_s[...] = qkv[:, 2 * DN_HEADS * DN_DK:]
    z_s[...] = _silu(_dot(hb, wz_ref[...]))
    ba = _dot(hb, wba_ref[...])
    beta_s[...] = jax.nn.sigmoid(ba)
    g_s[...] = -jnp.exp(alog_ref[...]) * jax.nn.softplus(ba + dtb_ref[...])

    c_ = DN_CHUNK
    row = lax.broadcasted_iota(jnp.int32, (c_, c_), 0)
    col = lax.broadcasted_iota(jnp.int32, (c_, c_), 1)
    tril = row >= col
    strict = row > col
    tril_f = tril.astype(F32)
    lane = lax.broadcasted_iota(jnp.int32, (c_, LANES), 1)
    normw = normw_ref[...]

    def chunk_body(ci, carry):
        r0 = pl.multiple_of(ci * c_, c_)
        rows = pl.ds(r0, c_)
        gcum = jnp.dot(tril_f, g_s[rows, :], precision=HIGHEST,
                       preferred_element_type=F32)
        glast = gcum[c_ - 1:c_, :]
        e_g = jnp.exp(gcum)
        e_tail = jnp.exp(glast - gcum)
        e_last = jnp.exp(glast)
        beta_c = beta_s[rows, :]
        for hd in range(DN_HEADS):
            lo = hd * DN_DK
            gl = DN_HEADS + hd
            qh = q_s[rows, lo:lo + DN_DK]
            kh = k_s[rows, lo:lo + DN_DK]
            vh = v_s[rows, lo:lo + DN_DV]
            bcol = beta_c[:, hd:hd + 1]
            egcol = e_g[:, gl:gl + 1]
            etcol = e_tail[:, gl:gl + 1]
            onehot = (lane == gl).astype(F32)
            grow = _dot_nt(onehot, gcum, precision=HIGHEST)
            decay = jnp.exp(jnp.where(tril, gcum[:, gl:gl + 1] - grow, NEG_INF))
            kb = kh * bcol
            kq = _dot_nt(jnp.concatenate([kb, qh], axis=0).astype(BF16), kh.astype(BF16))
            a_mat = jnp.where(strict, kq[:c_] * decay, 0.0)
            qk = jnp.where(tril, kq[c_:] * decay, 0.0)
            t_inv = _unit_lower_inverse(a_mat)
            rhs = jnp.concatenate([vh * bcol, kb * egcol], axis=1)
            sol = _dot(t_inv.astype(BF16), rhs.astype(BF16))
            u = sol[:, :DN_DV]
            w = sol[:, DN_DV:]
            s_old = state[hd]
            ws_qs = _dot(jnp.concatenate([w, qh * egcol], axis=0).astype(BF16), s_old.astype(BF16))
            v_new = u - ws_qs[:c_]
            v_new_b = v_new.astype(BF16)
            o = ws_qs[c_:] + _dot(qk.astype(BF16), v_new_b)
            state[hd] = s_old * e_last[:, gl:gl + 1] + _dot_tn((kh * etcol).astype(BF16), v_new_b)
            o = o * lax.rsqrt(jnp.mean(o * o, -1, keepdims=True) + RMS_EPS) * normw
            o = o * z_s[rows, lo:lo + DN_DV]
            odn_ref[0, rows, lo:lo + DN_DV] = o.astype(odn_ref.dtype)
        return carry

    lax.fori_loop(0, tm // c_, chunk_body, 0)

    w_ = WINDOW
    group = SWA_HEADS // SWA_KV_HEADS
    swq = _dot(hb, wswq_ref[...]) * (SWA_DH ** -0.5)
    kv = _dot(hb, wswkv_ref[...])
    lane_kv = lax.broadcasted_iota(jnp.int32, (tm, LANES), 1)
    first_half = lane_kv < SWA_DH
    for part in range(2):
        x = kv[:, part * LANES:(part + 1) * LANES]
        xr = pltpu.roll(x, SWA_DH, axis=1)
        dup0 = jnp.where(first_half, x, xr)
        dup1 = jnp.where(first_half, xr, x)
        kvbuf[w_:w_ + tm, (2 * part) * LANES:(2 * part + 1) * LANES] = dup0.astype(BF16)
        kvbuf[w_:w_ + tm, (2 * part + 1) * LANES:(2 * part + 2) * LANES] = dup1.astype(BF16)

    qi = lax.broadcasted_iota(jnp.int32, (group * w_, 2 * w_), 0) % w_
    kj = lax.broadcasted_iota(jnp.int32, (group * w_, 2 * w_), 1)
    band = (kj > qi) & (kj <= qi + w_)
    lane_q = lax.broadcasted_iota(jnp.int32, (w_, LANES), 1)
    lo_half = lane_q < SWA_DH
    for blk in range(tm // w_):
        mask = band & jnp.logical_or(kj >= w_, j > 0) if blk == 0 else band
        for kvh in range(SWA_KV_HEADS):
            kd = kvbuf[blk * w_:blk * w_ + 2 * w_, kvh * LANES:(kvh + 1) * LANES]
            vd = kvbuf[blk * w_:blk * w_ + 2 * w_, (2 + kvh) * LANES:(3 + kvh) * LANES]
            pieces, sink_cols = [], []
            for pair in range(group // 2):
                c0 = (kvh * (group // 2) + pair) * LANES
                qp = swq[blk * w_:(blk + 1) * w_, c0:c0 + LANES]
                pieces.append(jnp.where(lo_half, qp, 0.0))
                pieces.append(jnp.where(lo_half, 0.0, qp))
            for gi in range(group):
                sink_cols.append(jnp.full((w_, 1), sinks_ref[kvh * group + gi], F32))
            q4 = jnp.concatenate(pieces, axis=0).astype(BF16)
            sink = jnp.concatenate(sink_cols, axis=0)
            s = _dot_nt(q4, kd)
            s = jnp.where(mask, s, NEG_INF)
            m = jnp.maximum(jnp.max(s, axis=-1, keepdims=True), sink)
            p = jnp.exp(s - m)
            denom = jnp.sum(p, axis=-1, keepdims=True) + jnp.exp(sink - m)
            o4 = _dot(p.astype(BF16), vd) / denom
            for pair in range(group // 2):
                c0 = (kvh * (group // 2) + pair) * LANES
                o_pair = jnp.where(lo_half, o4[(2 * pair) * w_:(2 * pair + 1) * w_],
                                   o4[(2 * pair + 1) * w_:(2 * pair + 2) * w_])
                osw_ref[0, blk * w_:(blk + 1) * w_, c0:c0 + LANES] = o_pair.astype(osw_ref.dtype)
    kvbuf[0:w_, :] = kvbuf[tm:tm + w_, :]


def _mixers(h, wqkv, wba, wz, wswq, wswkv, convw, alog_row, dtb_row, normw_row, sinks):
    b, s, d = h.shape
    tm = min(TOKEN_TILE, s)
    n_qkv = 3 * DN_HEADS * DN_DK
    dn_w = DN_HEADS * DN_DV
    sw_w = SWA_HEADS * SWA_DH
    full = lambda shape: pl.BlockSpec(shape, lambda bi, ji: tuple(0 for _ in shape))
    tok = lambda width: pl.BlockSpec((1, tm, width), lambda bi, ji: (bi, ji, 0))
    return pl.pallas_call(
        functools.partial(_mixers_kernel, tm=tm),
        out_shape=(jax.ShapeDtypeStruct((b, s, dn_w), BF16), jax.ShapeDtypeStruct((b, s, sw_w), BF16)),
        grid=(b, s // tm),
        in_specs=[
            tok(d),
            full(wqkv.shape), full(wba.shape), full(wz.shape), full(wswq.shape), full(wswkv.shape),
            full(convw.shape), full(alog_row.shape), full(dtb_row.shape), full(normw_row.shape),
            pl.BlockSpec(memory_space=pltpu.SMEM),
        ],
        out_specs=(tok(dn_w), tok(sw_w)),
        scratch_shapes=[
            pltpu.VMEM((tm + SUBLANES, n_qkv), F32),
            pltpu.VMEM((tm, DN_HEADS * DN_DK), F32),
            pltpu.VMEM((tm, DN_HEADS * DN_DK), F32),
            pltpu.VMEM((tm, dn_w), F32),
            pltpu.VMEM((tm, dn_w), F32),
            pltpu.VMEM((tm, LANES), F32),
            pltpu.VMEM((tm, LANES), F32),
            pltpu.VMEM((DN_HEADS, DN_DK, DN_DV), F32),
            pltpu.VMEM((tm + WINDOW, 4 * LANES), BF16),
        ],
        compiler_params=pltpu.CompilerParams(
            dimension_semantics=("arbitrary", "arbitrary"), vmem_limit_bytes=VMEM_LIMIT_BYTES),
        name="mixers",
    )(h, wqkv, wba, wz, wswq, wswkv, convw, alog_row, dtb_row, normw_row, sinks)


def _merge_kernel(h_ref, odn_ref, osw_ref, mem_ref, memg_ref, memb_ref, wmkv_ref, wxq_ref, wg_ref,
                  wb_ref, wo_ref, g_ref, b_ref, o_ref, kmem, vmem, merged, *, alpha):
    j = pl.program_id(1)
    xa_w = XA_HEADS * XA_DH

    @pl.when(j == 0)
    def _memory_kv():
        mem_n = _layer_norm(mem_ref[0], memg_ref[...], memb_ref[...])
        kvm = _dot(mem_n.astype(BF16), wmkv_ref[...])
        kmem[...] = kvm[:, :xa_w].astype(BF16)
        vmem[...] = kvm[:, xa_w:].astype(BF16)

    hf = h_ref[0]
    hb = hf.astype(BF16)
    xq = _dot(hb, wxq_ref[...]) * (XA_DH ** -0.5)
    heads = []
    for hd in range(XA_HEADS):
        lo = hd * XA_DH
        s = _dot_nt(xq[:, lo:lo + XA_DH].astype(BF16), kmem[:, lo:lo + XA_DH])
        p = jnp.exp(s - jnp.max(s, axis=-1, keepdims=True))
        denom = jnp.sum(p, axis=-1, keepdims=True)
        heads.append(_dot(p.astype(BF16), vmem[:, lo:lo + XA_DH]) / denom)
    oxa = jnp.concatenate(heads, axis=1).astype(BF16)
    branches = (odn_ref[0], osw_ref[0], oxa)
    d = hf.shape[-1]
    for dc in range(d // MERGE_CHUNK):
        lo = dc * MERGE_CHUNK
        acc = None
        for n in range(N_BRANCH):
            gate = jax.nn.sigmoid(_dot(hb, wg_ref[n, :, lo:lo + MERGE_CHUNK]))
            term = gate * _dot(branches[n], wb_ref[n, :, lo:lo + MERGE_CHUNK])
            acc = term if acc is None else acc + term
        merged[:, lo:lo + MERGE_CHUNK] = acc.astype(BF16)
    y = alpha * hf + _dot(merged[...], wo_ref[...])
    o_ref[0] = _layer_norm(y, g_ref[...], b_ref[...])


def _merge(h, odn, osw, mem, memg, memb, wmkv, wxq, wgates, wbranch, wout, g, b_, alpha):
    b, s, d = h.shape
    tm = min(TOKEN_TILE, s)
    n_mem = mem.shape[1]
    xa_w = XA_HEADS * XA_DH
    full = lambda shape: pl.BlockSpec(shape, lambda bi, ji: tuple(0 for _ in shape))
    tok = lambda width: pl.BlockSpec((1, tm, width), lambda bi, ji: (bi, ji, 0))
    return pl.pallas_call(
        functools.partial(_merge_kernel, alpha=alpha),
        out_shape=jax.ShapeDtypeStruct((b, s, d), F32),
        grid=(b, s // tm),
        in_specs=[
            tok(d), tok(odn.shape[-1]), tok(osw.shape[-1]),
            pl.BlockSpec((1, n_mem, d), lambda bi, ji: (bi, 0, 0)),
            full(memg.shape), full(memb.shape), full(wmkv.shape), full(wxq.shape),
            full(wgates.shape), full(wbranch.shape), full(wout.shape), full(g.shape), full(b_.shape),
        ],
        out_specs=tok(d),
        scratch_shapes=[
            pltpu.VMEM((n_mem, xa_w), BF16),
            pltpu.VMEM((n_mem, xa_w), BF16),
            pltpu.VMEM((tm, d), BF16),
        ],
        compiler_params=pltpu.CompilerParams(
            dimension_semantics=("arbitrary", "arbitrary"), vmem_limit_bytes=VMEM_LIMIT_BYTES),
        name="merge",
    )(h, odn, osw, mem, memg, memb, wmkv, wxq, wgates, wbranch, wout, g, b_)


def _ffn_weights(w_gu, w_down):
    d, two_ff = w_gu.shape
    ff = two_ff // 2
    nc = ff // FF_CHUNK
    wg = w_gu[:, :ff].astype(BF16).reshape(d, nc, FF_CHUNK).transpose(1, 0, 2)
    wu = w_gu[:, ff:].astype(BF16).reshape(d, nc, FF_CHUNK).transpose(1, 0, 2)
    wd = w_down.astype(BF16).reshape(nc, FF_CHUNK, d)
    return wg, wu, wd


def _lane_row(vals, offset):
    return jnp.zeros((1, LANES), F32).at[0, offset:offset + vals.shape[0]].set(vals.astype(F32))


def kernel(x, mem, mem_ln_g, mem_ln_b, ln_g, ln_b, ffn1_w_gu, ffn1_w_down, w_in, dn_conv_w, dn_a_log,
           dn_dt_bias, dn_norm_w, swa_sinks, w_mem_kv, w_branch, w_out, ffn2_w_gu, ffn2_w_down):
    b, s, d = x.shape
    depth = ln_g.shape[0]
    alpha = float((2 * depth) ** 0.25)
    n_dn = DN_HEADS * DN_DK
    o_b = 3 * n_dn
    o_a = o_b + DN_HEADS
    o_z = o_a + DN_HEADS
    o_swq = o_z + DN_HEADS * DN_DV
    o_swk = o_swq + SWA_HEADS * SWA_DH
    o_xaq = o_swk + 2 * SWA_KV_HEADS * SWA_DH
    o_gate = o_xaq + XA_HEADS * XA_DH
    row = lambda v: v.reshape(1, -1).astype(F32)

    for l in range(depth):
        wi = w_in[l]
        wba = jnp.zeros((d, LANES), F32).at[:, :2 * DN_HEADS].set(wi[:, o_b:o_z]).astype(BF16)
        wgates = wi[:, o_gate:].astype(BF16).reshape(d, N_BRANCH, d).transpose(1, 0, 2)

        wg, wu, wd = _ffn_weights(ffn1_w_gu[l], ffn1_w_down[l])
        h = _ffn_ln(x.reshape(b * s, d), wg, wu, wd, row(ln_g[l, 0]), row(ln_b[l, 0]), alpha)
        h = h.reshape(b, s, d)

        odn, osw = _mixers(
            h, wi[:, :3 * n_dn].astype(BF16), wba, wi[:, o_z:o_swq].astype(BF16),
            wi[:, o_swq:o_swk].astype(BF16), wi[:, o_swk:o_xaq].astype(BF16),
            dn_conv_w[l].astype(F32), _lane_row(dn_a_log[l], DN_HEADS), _lane_row(dn_dt_bias[l], DN_HEADS),
            row(dn_norm_w[l]), swa_sinks[l].astype(F32))

        h = _merge(h, odn, osw, mem, row(mem_ln_g), row(mem_ln_b), w_mem_kv[l].astype(BF16),
                   wi[:, o_xaq:o_gate].astype(BF16), wgates, w_branch[l].astype(BF16),
                   w_out[l].astype(BF16), row(ln_g[l, 1]), row(ln_b[l, 1]), alpha)

        wg, wu, wd = _ffn_weights(ffn2_w_gu[l], ffn2_w_down[l])
        x = _ffn_ln(h.reshape(b * s, d), wg, wu, wd, row(ln_g[l, 2]), row(ln_b[l, 2]), alpha)
        x = x.reshape(b, s, d)
    return x
```

```python
import functools

import jax
import jax.numpy as jnp
from jax import lax
from jax.experimental import pallas as pl
from jax.experimental.pallas import tpu as pltpu

F32 = jnp.float32
BF16 = jnp.bfloat16
HIGHEST = lax.Precision.HIGHEST

DN_HEADS, DN_DK, DN_DV, DN_CONV, DN_CHUNK = 4, 128, 128, 4, 64
SWA_HEADS, SWA_KV_HEADS, SWA_DH, WINDOW = 8, 2, 64, 128
XA_HEADS, XA_DH = 4, 128
N_BRANCH, BRANCH_W = 3, 512
LN_EPS, RMS_EPS, NEG_INF = 1e-5, 1e-6, -1e30

LANES = 128
SUBLANES = 8
VMEM_LIMIT_BYTES = 56 * 1024 * 1024

TOKEN_TILE = 512
FF_CHUNK = 256
MERGE_CHUNK = 256


def _dot(a, b):
    return jnp.dot(a, b, preferred_element_type=F32)


def _dot_nt(a, b, precision=None):
    return lax.dot_general(a, b, (((1,), (1,)), ((), ())), precision=precision,
                           preferred_element_type=F32)


def _dot_tn(a, b):
    return lax.dot_general(a, b, (((0,), (0,)), ((), ())), preferred_element_type=F32)


def _layer_norm(y, g, b):
    mu = jnp.mean(y, axis=-1, keepdims=True)
    d = y - mu
    var = jnp.mean(d * d, axis=-1, keepdims=True)
    return d * lax.rsqrt(var + LN_EPS) * g + b


def _silu(x):
    return x * jax.nn.sigmoid(x)


def _ffn_ln_kernel(x_ref, wg_ref, wu_ref, wd_ref, g_ref, b_ref, o_ref, acc_ref, *, alpha, n_chunks):
    xf = x_ref[...]
    xb = xf.astype(BF16)
    for c in range(n_chunks):
        gate = _dot(xb, wg_ref[c])
        up = _dot(xb, wu_ref[c])
        act = (_silu(gate) * up).astype(BF16)
        part = _dot(act, wd_ref[c])
        if c == 0:
            acc_ref[...] = part
        else:
            acc_ref[...] += part
    y = alpha * xf + 0.5 * acc_ref[...]
    o_ref[...] = _layer_norm(y, g_ref[...], b_ref[...])


def _ffn_ln(x2d, wg3, wu3, wd3, g, b, alpha):
    t, d = x2d.shape
    n_chunks, _, fc = wg3.shape
    tm = min(TOKEN_TILE, t)
    const3 = lambda i: (0, 0, 0)
    return pl.pallas_call(
        functools.partial(_ffn_ln_kernel, alpha=alpha, n_chunks=n_chunks),
        out_shape=jax.ShapeDtypeStruct((t, d), F32),
        grid=(t // tm,),
        in_specs=[
            pl.BlockSpec((tm, d), lambda i: (i, 0)),
            pl.BlockSpec((n_chunks, d, fc), const3),
            pl.BlockSpec((n_chunks, d, fc), const3),
            pl.BlockSpec((n_chunks, fc, d), const3),
            pl.BlockSpec((1, d), lambda i: (0, 0)),
            pl.BlockSpec((1, d), lambda i: (0, 0)),
        ],
        out_specs=pl.BlockSpec((tm, d), lambda i: (i, 0)),
        scratch_shapes=[pltpu.VMEM((tm, d), F32)],
        compiler_params=pltpu.CompilerParams(
            dimension_semantics=("arbitrary",), vmem_limit_bytes=VMEM_LIMIT_BYTES),
        name="ffn_ln",
    )(x2d, wg3, wu3, wd3, g, b)


def _split_bf16(x):
    hi = x.astype(BF16)
    lo = (x - hi.astype(F32)).astype(BF16)
    return hi, lo


def _block_diag(y_l, n_blocks):
    c = y_l.shape[0]
    lane_block = lax.broadcasted_iota(jnp.int32, y_l.shape, 1) // c
    zero = jnp.zeros_like(y_l)
    return jnp.concatenate([jnp.where(lane_block == b, y_l, zero) for b in range(n_blocks)], axis=0)


def _blockwise_matmul(l_l, y_l, n_blocks):
    lh, ll = _split_bf16(l_l)
    yh, yl = _split_bf16(y_l)
    m = l_l.shape[0]
    top = _dot(jnp.concatenate([lh, ll], axis=0), _block_diag(yh, n_blocks))
    return top[:m] + top[m:] + _dot(lh, _block_diag(yl, n_blocks))


def _unit_lower_inverses(a_ls, n_blocks):
    c = a_ls[0].shape[0]
    row = lax.broadcasted_iota(jnp.int32, a_ls[0].shape, 0)
    col = lax.broadcasted_iota(jnp.int32, a_ls[0].shape, 1) % c
    eye = (row == col).astype(F32)
    ps = [eye - a_l for a_l in a_ls]
    ys = [_blockwise_matmul(-a_l, -a_l, n_blocks) for a_l in a_ls]
    power = 2
    while power < c:
        if 2 * power < c:
            stacked = [_blockwise_matmul(jnp.concatenate([p, y], axis=0), y, n_blocks)
                       for p, y in zip(ps, ys)]
            ps = [p + s[:c] for p, s in zip(ps, stacked)]
            ys = [s[c:] for s in stacked]
        else:
            ps = [p + _blockwise_matmul(p, y, n_blocks) for p, y in zip(ps, ys)]
        power *= 2
    return ps


def _mixers_kernel(h_ref, wqkv_ref, wba_ref, wz_ref, wswq_ref, wswkv_ref, convw_ref, alog_ref,
                   dtb_ref, normw_ref, sinks_ref, odn_ref, osw_ref,
                   xbuf, q_s, k_s, kb_s, qd_s, kt_s, rhs_s, z_s, gcum_s, gcol_s, elast_s, qk_s, u_s, w_s, state, kvbuf,
                   *, tm):
    j = pl.program_id(1)
    n_qkv = 3 * DN_HEADS * DN_DK
    n_dn = DN_HEADS * DN_DK
    halo = SUBLANES
    c_ = DN_CHUNK
    n_chunks = tm // c_
    gate_lane = DN_HEADS

    @pl.when(j == 0)
    def _reset():
        xbuf[0:halo, :] = jnp.zeros((halo, n_qkv), F32)
        state[...] = jnp.zeros_like(state)
        kvbuf[0:WINDOW, :] = jnp.zeros((WINDOW, kvbuf.shape[1]), BF16)

    hb = h_ref[0].astype(BF16)

    pre = _dot(hb, wqkv_ref[...])
    xbuf[halo:halo + tm, :] = pre
    cw = convw_ref[...]
    conv = cw[DN_CONV - 1:DN_CONV, :] * pre
    for s in range(1, DN_CONV):
        conv = conv + cw[DN_CONV - 1 - s:DN_CONV - s, :] * xbuf[halo - s:halo - s + tm, :]
    xbuf[0:halo, :] = xbuf[tm:tm + halo, :]
    qkv = _silu(conv)
    z_s[...] = _silu(_dot(hb, wz_ref[...]))

    ba = _dot(hb, wba_ref[...])
    beta = jax.nn.sigmoid(ba[:, :LANES])
    g = -jnp.exp(alog_ref[...]) * jax.nn.softplus(ba[:, LANES:] + dtb_ref[...])
    row_c = lax.broadcasted_iota(jnp.int32, (c_, c_), 0)
    col_c = lax.broadcasted_iota(jnp.int32, (c_, c_), 1)
    tril_f = (row_c >= col_c).astype(F32)
    gcum_parts, etail_parts, elast_parts = [], [], []
    for ci in range(n_chunks):
        gc = jnp.dot(tril_f, g[ci * c_:(ci + 1) * c_, :], precision=HIGHEST,
                     preferred_element_type=F32)
        glast = gc[c_ - 1:c_, :]
        gcum_parts.append(gc)
        etail_parts.append(jnp.exp(glast - gc))
        elast_parts.append(jnp.broadcast_to(jnp.exp(glast), (c_, LANES)))
    gcum = jnp.concatenate(gcum_parts, axis=0)
    gcum_s[...] = gcum
    elast_s[...] = jnp.concatenate(elast_parts, axis=0)
    sel_k = lax.broadcasted_iota(jnp.int32, (LANES, DN_HEADS * c_), 0)
    sel_n = lax.broadcasted_iota(jnp.int32, (LANES, DN_HEADS * c_), 1)
    gcol_s[...] = jnp.dot(gcum, (sel_k == gate_lane + sel_n // c_).astype(F32), precision=HIGHEST,
                          preferred_element_type=F32)
    bk = lax.broadcasted_iota(jnp.int32, (LANES, n_dn), 0)
    bn = lax.broadcasted_iota(jnp.int32, (LANES, n_dn), 1)
    spread = (bk == gate_lane + bn // DN_DK).astype(BF16)
    stacked = jnp.concatenate([beta, jnp.exp(gcum), jnp.concatenate(etail_parts, axis=0)], axis=0)
    bc = _dot(stacked.astype(BF16), spread)
    for hd in range(DN_HEADS):
        lo = hd * DN_DK
        hs = slice(lo, lo + DN_DK)
        qh = qkv[:, lo:lo + DN_DK]
        qh = qh * (lax.rsqrt(jnp.sum(qh * qh, -1, keepdims=True) + RMS_EPS) * (DN_DK ** -0.5))
        kh = qkv[:, n_dn + lo:n_dn + lo + DN_DK]
        kh = kh * lax.rsqrt(jnp.sum(kh * kh, -1, keepdims=True) + RMS_EPS)
        vh = qkv[:, 2 * n_dn + lo:2 * n_dn + lo + DN_DV]
        beta_b = bc[0:tm, hs]
        eg_b = bc[tm:2 * tm, hs]
        et_b = bc[2 * tm:3 * tm, hs]
        kb = kh * beta_b
        q_s[:, hs] = qh.astype(BF16)
        k_s[:, hs] = kh.astype(BF16)
        kb_s[:, hs] = kb.astype(BF16)
        qd_s[:, hs] = (qh * eg_b).astype(BF16)
        kt_s[:, hs] = (kh * et_b).astype(BF16)
        rhs_s[:, 2 * lo:2 * lo + DN_DV] = (vh * beta_b).astype(BF16)
        rhs_s[:, 2 * lo + DN_DV:2 * lo + DN_DV + DN_DK] = (kb * eg_b).astype(BF16)

    stack_w = DN_HEADS * c_
    row_l = lax.broadcasted_iota(jnp.int32, (c_, stack_w), 0)
    col_l = lax.broadcasted_iota(jnp.int32, (c_, stack_w), 1) % c_
    tril_l = row_l >= col_l
    strict_l = row_l > col_l
    lane = lax.broadcasted_iota(jnp.int32, (c_, LANES), 1)
    head_block = lax.broadcasted_iota(jnp.int32, (c_, n_dn), 1) // DN_DK
    ones_l = jnp.ones((c_, LANES), F32)
    normw = normw_ref[...]
    zero_rhs = jnp.zeros((c_, DN_DV + DN_DK), BF16)
    zero_v = jnp.zeros((c_, DN_DV), BF16)

    chunk_rows = [slice(ci * c_, (ci + 1) * c_) for ci in range(n_chunks)]
    decays = []
    for rows in chunk_rows:
        gc = gcum_s[rows, :]
        g_rows = jnp.concatenate(
            [jnp.where(lane == gate_lane + hd, gc, 0.0) for hd in range(DN_HEADS)], axis=0)
        grow_l = _dot_nt(ones_l, g_rows, precision=HIGHEST)
        decays.append(jnp.exp(jnp.where(tril_l, gcol_s[rows, :] - grow_l, NEG_INF)))
    kqs = []
    for rows in chunk_rows:
        k_rows = k_s[rows, :]
        k_bd = jnp.concatenate(
            [jnp.where(head_block == hd, k_rows, jnp.zeros_like(k_rows)) for hd in range(DN_HEADS)],
            axis=0)
        kqs.append(_dot_nt(jnp.concatenate([kb_s[rows, :], q_s[rows, :]], axis=0), k_bd))
    a_ls = [jnp.where(strict_l, kq[:c_] * dec, 0.0) for kq, dec in zip(kqs, decays)]
    for rows, kq, dec in zip(chunk_rows, kqs, decays):
        qk_s[rows, :] = jnp.where(tril_l, kq[c_:] * dec, 0.0).astype(BF16)
    t_invs = _unit_lower_inverses(a_ls, DN_HEADS)
    for rows, t_inv in zip(chunk_rows, t_invs):
        t_inv_b = t_inv.astype(BF16)
        for hd in range(DN_HEADS):
            lo = hd * DN_DK
            pair = slice((hd // 2) * LANES, (hd // 2 + 1) * LANES)
            rhs_h = rhs_s[rows, 2 * lo:2 * lo + DN_DV + DN_DK]
            rhs_pad = ([rhs_h, zero_rhs] if hd % 2 == 0 else [zero_rhs, rhs_h])
            sol = _dot(t_inv_b[:, pair], jnp.concatenate(rhs_pad, axis=0))
            u_s[rows, lo:lo + DN_DV] = sol[:, :DN_DV]
            w_s[rows, lo:lo + DN_DK] = sol[:, DN_DV:].astype(BF16)

    def chunk_body(ci, carry):
        r0 = pl.multiple_of(ci * c_, c_)
        rows = pl.ds(r0, c_)
        e_last = elast_s[pl.ds(r0, 1), :]
        heads = range(DN_HEADS)
        hsl = [slice(hd * DN_DK, (hd + 1) * DN_DK) for hd in heads]
        pairs = [slice((hd // 2) * LANES, (hd // 2 + 1) * LANES) for hd in heads]
        s_old = [state[hd] for hd in heads]
        ws_qs = [_dot(jnp.concatenate([w_s[rows, hsl[hd]], qd_s[rows, hsl[hd]]], axis=0),
                      s_old[hd].astype(BF16)) for hd in heads]
        v_new_b = [(u_s[rows, hsl[hd]] - ws_qs[hd][:c_]).astype(BF16) for hd in heads]
        for hd in heads:
            state[hd] = (s_old[hd] * e_last[:, gate_lane + hd:gate_lane + hd + 1]
                         + _dot_tn(kt_s[rows, hsl[hd]], v_new_b[hd]))
        for hd in heads:
            v_pad = ([v_new_b[hd], zero_v] if hd % 2 == 0 else [zero_v, v_new_b[hd]])
            o = ws_qs[hd][c_:] + _dot(qk_s[rows, pairs[hd]], jnp.concatenate(v_pad, axis=0))
            o = o * lax.rsqrt(jnp.mean(o * o, -1, keepdims=True) + RMS_EPS) * normw
            o = o * z_s[rows, hsl[hd]]
            odn_ref[0, rows, hsl[hd]] = o.astype(odn_ref.dtype)
        return carry

    lax.fori_loop(0, n_chunks, chunk_body, 0)

    w_ = WINDOW
    group = SWA_HEADS // SWA_KV_HEADS
    swq = _dot(hb, wswq_ref[...]) * (SWA_DH ** -0.5)
    kv = _dot(hb, wswkv_ref[...])
    lane_kv = lax.broadcasted_iota(jnp.int32, (tm, LANES), 1)
    first_half = lane_kv < SWA_DH
    for part in range(2):
        x = kv[:, part * LANES:(part + 1) * LANES]
        xr = pltpu.roll(x, SWA_DH, axis=1)
        dup0 = jnp.where(first_half, x, xr)
        dup1 = jnp.where(first_half, xr, x)
        kvbuf[w_:w_ + tm, (2 * part) * LANES:(2 * part + 1) * LANES] = dup0.astype(BF16)
        kvbuf[w_:w_ + tm, (2 * part + 1) * LANES:(2 * part + 2) * LANES] = dup1.astype(BF16)

    qi = lax.broadcasted_iota(jnp.int32, (group * w_, 2 * w_), 0) % w_
    kj = lax.broadcasted_iota(jnp.int32, (group * w_, 2 * w_), 1)
    band = (kj > qi) & (kj <= qi + w_)
    lane_q = lax.broadcasted_iota(jnp.int32, (w_, LANES), 1)
    lo_half = lane_q < SWA_DH
    sinks4 = [jnp.concatenate([jnp.full((w_, 1), sinks_ref[kvh * group + gi], F32)
                               for gi in range(group)], axis=0) for kvh in range(SWA_KV_HEADS)]
    problems = [(blk, kvh) for blk in range(tm // w_) for kvh in range(SWA_KV_HEADS)]
    scores = []
    for blk, kvh in problems:
        kd = kvbuf[blk * w_:blk * w_ + 2 * w_, kvh * LANES:(kvh + 1) * LANES]
        pieces = []
        for pair in range(group // 2):
            c0 = (kvh * (group // 2) + pair) * LANES
            qp = swq[blk * w_:(blk + 1) * w_, c0:c0 + LANES]
            pieces.append(jnp.where(lo_half, qp, 0.0))
            pieces.append(jnp.where(lo_half, 0.0, qp))
        q4 = jnp.concatenate(pieces, axis=0).astype(BF16)
        mask = band & jnp.logical_or(kj >= w_, j > 0) if blk == 0 else band
        scores.append(jnp.where(mask, _dot_nt(q4, kd), NEG_INF))
    maxes = [jnp.maximum(jnp.max(s, axis=-1, keepdims=True), sinks4[kvh])
             for s, (blk, kvh) in zip(scores, problems)]
    probs = [jnp.exp(s - m) for s, m in zip(scores, maxes)]
    denoms = [jnp.sum(p, axis=-1, keepdims=True) + jnp.exp(sinks4[kvh] - m)
              for p, m, (blk, kvh) in zip(probs, maxes, problems)]
    for p, denom, (blk, kvh) in zip(probs, denoms, problems):
        vd = kvbuf[blk * w_:blk * w_ + 2 * w_, (2 + kvh) * LANES:(3 + kvh) * LANES]
        o4 = _dot(p.astype(BF16), vd) / denom
        for pair in range(group // 2):
            c0 = (kvh * (group // 2) + pair) * LANES
            o_pair = jnp.where(lo_half, o4[(2 * pair) * w_:(2 * pair + 1) * w_],
                               o4[(2 * pair + 1) * w_:(2 * pair + 2) * w_])
            osw_ref[0, blk * w_:(blk + 1) * w_, c0:c0 + LANES] = o_pair.astype(osw_ref.dtype)
    kvbuf[0:w_, :] = kvbuf[tm:tm + w_, :]


def _mixers(h, wqkv, wba, wz, wswq, wswkv, convw, alog_row, dtb_row, normw_row, sinks):
    b, s, d = h.shape
    tm = min(TOKEN_TILE, s)
    n_qkv = 3 * DN_HEADS * DN_DK
    dn_w = DN_HEADS * DN_DV
    sw_w = SWA_HEADS * SWA_DH
    full = lambda shape: pl.BlockSpec(shape, lambda bi, ji: tuple(0 for _ in shape))
    tok = lambda width: pl.BlockSpec((1, tm, width), lambda bi, ji: (bi, ji, 0))
    return pl.pallas_call(
        functools.partial(_mixers_kernel, tm=tm),
        out_shape=(jax.ShapeDtypeStruct((b, s, dn_w), BF16), jax.ShapeDtypeStruct((b, s, sw_w), BF16)),
        grid=(b, s // tm),
        in_specs=[
            tok(d),
            full(wqkv.shape), full(wba.shape), full(wz.shape), full(wswq.shape), full(wswkv.shape),
            full(convw.shape), full(alog_row.shape), full(dtb_row.shape), full(normw_row.shape),
            pl.BlockSpec(memory_space=pltpu.SMEM),
        ],
        out_specs=(tok(dn_w), tok(sw_w)),
        scratch_shapes=[
            pltpu.VMEM((tm + SUBLANES, n_qkv), F32),
            pltpu.VMEM((tm, dn_w), BF16),
            pltpu.VMEM((tm, dn_w), BF16),
            pltpu.VMEM((tm, dn_w), BF16),
            pltpu.VMEM((tm, dn_w), BF16),
            pltpu.VMEM((tm, dn_w), BF16),
            pltpu.VMEM((tm, 2 * dn_w), BF16),
            pltpu.VMEM((tm, dn_w), F32),
            pltpu.VMEM((tm, LANES), F32),
            pltpu.VMEM((tm, DN_HEADS * DN_CHUNK), F32),
            pltpu.VMEM((tm, LANES), F32),
            pltpu.VMEM((tm, DN_HEADS * DN_CHUNK), BF16),
            pltpu.VMEM((tm, dn_w), F32),
            pltpu.VMEM((tm, dn_w), BF16),
            pltpu.VMEM((DN_HEADS, DN_DK, DN_DV), F32),
            pltpu.VMEM((tm + WINDOW, 4 * LANES), BF16),
        ],
        compiler_params=pltpu.CompilerParams(
            dimension_semantics=("arbitrary", "arbitrary"), vmem_limit_bytes=VMEM_LIMIT_BYTES),
        name="mixers",
    )(h, wqkv, wba, wz, wswq, wswkv, convw, alog_row, dtb_row, normw_row, sinks)


def _merge_kernel(h_ref, odn_ref, osw_ref, mem_ref, memg_ref, memb_ref, wmkv_ref, wxq_ref, wg_ref,
                  wb_ref, wo_ref, g_ref, b_ref, o_ref, kmem, vmem, merged, *, alpha):
    j = pl.program_id(1)
    xa_w = XA_HEADS * XA_DH

    @pl.when(j == 0)
    def _memory_kv():
        mem_n = _layer_norm(mem_ref[0], memg_ref[...], memb_ref[...])
        kvm = _dot(mem_n.astype(BF16), wmkv_ref[...])
        kmem[...] = kvm[:, :xa_w].astype(BF16)
        vmem[...] = kvm[:, xa_w:].astype(BF16)

    hf = h_ref[0]
    hb = hf.astype(BF16)
    xq = _dot(hb, wxq_ref[...]) * (XA_DH ** -0.5)
    heads = []
    for hd in range(XA_HEADS):
        lo = hd * XA_DH
        s = _dot_nt(xq[:, lo:lo + XA_DH].astype(BF16), kmem[:, lo:lo + XA_DH])
        p = jnp.exp(s - jnp.max(s, axis=-1, keepdims=True))
        denom = jnp.sum(p, axis=-1, keepdims=True)
        heads.append(_dot(p.astype(BF16), vmem[:, lo:lo + XA_DH]) / denom)
    oxa = jnp.concatenate(heads, axis=1).astype(BF16)
    branches = (odn_ref[0], osw_ref[0], oxa)
    d = hf.shape[-1]
    for dc in range(d // MERGE_CHUNK):
        lo = dc * MERGE_CHUNK
        acc = None
        for n in range(N_BRANCH):
            gate = jax.nn.sigmoid(_dot(hb, wg_ref[n, :, lo:lo + MERGE_CHUNK]))
            term = gate * _dot(branches[n], wb_ref[n, :, lo:lo + MERGE_CHUNK])
            acc = term if acc is None else acc + term
        merged[:, lo:lo + MERGE_CHUNK] = acc.astype(BF16)
    y = alpha * hf + _dot(merged[...], wo_ref[...])
    o_ref[0] = _layer_norm(y, g_ref[...], b_ref[...])


def _merge(h, odn, osw, mem, memg, memb, wmkv, wxq, wgates, wbranch, wout, g, b_, alpha):
    b, s, d = h.shape
    tm = min(TOKEN_TILE, s)
    n_mem = mem.shape[1]
    xa_w = XA_HEADS * XA_DH
    full = lambda shape: pl.BlockSpec(shape, lambda bi, ji: tuple(0 for _ in shape))
    tok = lambda width: pl.BlockSpec((1, tm, width), lambda bi, ji: (bi, ji, 0))
    return pl.pallas_call(
        functools.partial(_merge_kernel, alpha=alpha),
        out_shape=jax.ShapeDtypeStruct((b, s, d), F32),
        grid=(b, s // tm),
        in_specs=[
            tok(d), tok(odn.shape[-1]), tok(osw.shape[-1]),
            pl.BlockSpec((1, n_mem, d), lambda bi, ji: (bi, 0, 0)),
            full(memg.shape), full(memb.shape), full(wmkv.shape), full(wxq.shape),
            full(wgates.shape), full(wbranch.shape), full(wout.shape), full(g.shape), full(b_.shape),
        ],
        out_specs=tok(d),
        scratch_shapes=[
            pltpu.VMEM((n_mem, xa_w), BF16),
            pltpu.VMEM((n_mem, xa_w), BF16),
            pltpu.VMEM((tm, d), BF16),
        ],
        compiler_params=pltpu.CompilerParams(
            dimension_semantics=("arbitrary", "arbitrary"), vmem_limit_bytes=VMEM_LIMIT_BYTES),
        name="merge",
    )(h, odn, osw, mem, memg, memb, wmkv, wxq, wgates, wbranch, wout, g, b_)


def _ffn_weights(w_gu, w_down):
    d, two_ff = w_gu.shape
    ff = two_ff // 2
    nc = ff // FF_CHUNK
    wg = w_gu[:, :ff].astype(BF16).reshape(d, nc, FF_CHUNK).transpose(1, 0, 2)
    wu = w_gu[:, ff:].astype(BF16).reshape(d, nc, FF_CHUNK).transpose(1, 0, 2)
    wd = w_down.astype(BF16).reshape(nc, FF_CHUNK, d)
    return wg, wu, wd


def _lane_row(vals, offset):
    return jnp.zeros((1, LANES), F32).at[0, offset:offset + vals.shape[0]].set(vals.astype(F32))


def _mixer_params(wi, conv_w, a_log, dt_bias, norm_w, sinks):
    d = wi.shape[0]
    n_dn = DN_HEADS * DN_DK
    o_b = 3 * n_dn
    o_a = o_b + DN_HEADS
    o_z = o_a + DN_HEADS
    o_swq = o_z + DN_HEADS * DN_DV
    o_swk = o_swq + SWA_HEADS * SWA_DH
    o_xaq = o_swk + 2 * SWA_KV_HEADS * SWA_DH
    wba = (jnp.zeros((d, 2 * LANES), F32)
           .at[:, DN_HEADS:2 * DN_HEADS].set(wi[:, o_b:o_a])
           .at[:, LANES + DN_HEADS:LANES + 2 * DN_HEADS].set(wi[:, o_a:o_z]).astype(BF16))
    return (wi[:, :o_b].astype(BF16), wba, wi[:, o_z:o_swq].astype(BF16),
            wi[:, o_swq:o_swk].astype(BF16), wi[:, o_swk:o_xaq].astype(BF16),
            conv_w.astype(F32), _lane_row(a_log, DN_HEADS), _lane_row(dt_bias, DN_HEADS),
            norm_w.reshape(1, -1).astype(F32), sinks.astype(F32))


def kernel(x, mem, mem_ln_g, mem_ln_b, ln_g, ln_b, ffn1_w_gu, ffn1_w_down, w_in, dn_conv_w, dn_a_log,
           dn_dt_bias, dn_norm_w, swa_sinks, w_mem_kv, w_branch, w_out, ffn2_w_gu, ffn2_w_down):
    b, s, d = x.shape
    depth = ln_g.shape[0]
    alpha = float((2 * depth) ** 0.25)
    o_xaq = w_in.shape[-1] - N_BRANCH * d - XA_HEADS * XA_DH
    o_gate = o_xaq + XA_HEADS * XA_DH
    row = lambda v: v.reshape(1, -1).astype(F32)

    for l in range(depth):
        wi = w_in[l]
        wgates = wi[:, o_gate:].astype(BF16).reshape(d, N_BRANCH, d).transpose(1, 0, 2)

        wg, wu, wd = _ffn_weights(ffn1_w_gu[l], ffn1_w_down[l])
        h = _ffn_ln(x.reshape(b * s, d), wg, wu, wd, row(ln_g[l, 0]), row(ln_b[l, 0]), alpha)
        h = h.reshape(b, s, d)

        odn, osw = _mixers(h, *_mixer_params(wi, dn_conv_w[l], dn_a_log[l], dn_dt_bias[l],
                                             dn_norm_w[l], swa_sinks[l]))

        h = _merge(h, odn, osw, mem, row(mem_ln_g), row(mem_ln_b), w_mem_kv[l].astype(BF16),
                   wi[:, o_xaq:o_gate].astype(BF16), wgates, w_branch[l].astype(BF16),
                   w_out[l].astype(BF16), row(ln_g[l, 1]), row(ln_b[l, 1]), alpha)

        wg, wu, wd = _ffn_weights(ffn2_w_gu[l], ffn2_w_down[l])
        x = _ffn_ln(h.reshape(b * s, d), wg, wu, wd, row(ln_g[l, 2]), row(ln_b[l, 2]), alpha)
        x = x.reshape(b, s, d)
    return x
```

```python
import functools

import jax
import jax.numpy as jnp
from jax import lax
from jax.experimental import pallas as pl
from jax.experimental.pallas import tpu as pltpu

F32 = jnp.float32
BF16 = jnp.bfloat16
HIGHEST = lax.Precision.HIGHEST

DN_HEADS, DN_DK, DN_DV, DN_CONV, DN_CHUNK = 4, 128, 128, 4, 64
SWA_HEADS, SWA_KV_HEADS, SWA_DH, WINDOW = 8, 2, 64, 128
XA_HEADS, XA_DH = 4, 128
N_BRANCH, BRANCH_W = 3, 512
LN_EPS, RMS_EPS, NEG_INF = 1e-5, 1e-6, -1e30

LANES = 128
SUBLANES = 8
VMEM_LIMIT_BYTES = 56 * 1024 * 1024

TOKEN_TILE = 512
FF_CHUNK = 256
MERGE_CHUNK = 256
WY_GROUP = 8

_N_DN = DN_HEADS * DN_DK
MIX_QKV = slice(0, 3 * _N_DN)
MIX_Z = slice(MIX_QKV.stop, MIX_QKV.stop + DN_HEADS * DN_DV)
MIX_SWQ = slice(MIX_Z.stop, MIX_Z.stop + SWA_HEADS * SWA_DH)
MIX_SWKV = slice(MIX_SWQ.stop, MIX_SWQ.stop + 2 * SWA_KV_HEADS * SWA_DH)
MIX_BA = slice(MIX_SWKV.stop, MIX_SWKV.stop + 2 * LANES)
MIX_WIDTH = MIX_BA.stop
MRG_XAQ = slice(0, XA_HEADS * XA_DH)
MRG_GATE0 = MRG_XAQ.stop


def _dot(a, b):
    return jnp.dot(a, b, preferred_element_type=F32)


def _dot_nt(a, b, precision=None):
    return lax.dot_general(a, b, (((1,), (1,)), ((), ())), precision=precision,
                           preferred_element_type=F32)


def _dot_tn(a, b):
    return lax.dot_general(a, b, (((0,), (0,)), ((), ())), preferred_element_type=F32)


def _layer_norm(y, g, b):
    mu = jnp.mean(y, axis=-1, keepdims=True)
    d = y - mu
    var = jnp.mean(d * d, axis=-1, keepdims=True)
    return d * lax.rsqrt(var + LN_EPS) * g + b


def _silu(x):
    return x * jax.nn.sigmoid(x)


def _ffn_ln_kernel(x_ref, wgu_ref, wd_ref, g_ref, b_ref, o_ref, acc_ref, *, alpha):
    d_ff = wd_ref.shape[0]
    xf = x_ref[...]
    xb = xf.astype(BF16)
    for c in range(d_ff // FF_CHUNK):
        lo = c * FF_CHUNK
        gate = _dot(xb, wgu_ref[:, lo:lo + FF_CHUNK])
        up = _dot(xb, wgu_ref[:, d_ff + lo:d_ff + lo + FF_CHUNK])
        act = (_silu(gate) * up).astype(BF16)
        part = _dot(act, wd_ref[lo:lo + FF_CHUNK, :])
        if c == 0:
            acc_ref[...] = part
        else:
            acc_ref[...] += part
    y = alpha * xf + 0.5 * acc_ref[...]
    o_ref[...] = _layer_norm(y, g_ref[...], b_ref[...])


def _ffn_ln(x2d, w_gu, w_down, layer, g, b, alpha):
    t, d = x2d.shape
    _, d_ff, _ = w_down.shape
    tm = min(TOKEN_TILE, t)
    return pl.pallas_call(
        functools.partial(_ffn_ln_kernel, alpha=alpha),
        out_shape=jax.ShapeDtypeStruct((t, d), F32),
        grid=(t // tm,),
        in_specs=[
            pl.BlockSpec((tm, d), lambda i: (i, 0)),
            pl.BlockSpec((None, d, 2 * d_ff), lambda i: (layer, 0, 0)),
            pl.BlockSpec((None, d_ff, d), lambda i: (layer, 0, 0)),
            pl.BlockSpec((1, d), lambda i: (0, 0)),
            pl.BlockSpec((1, d), lambda i: (0, 0)),
        ],
        out_specs=pl.BlockSpec((tm, d), lambda i: (i, 0)),
        scratch_shapes=[pltpu.VMEM((tm, d), F32)],
        compiler_params=pltpu.CompilerParams(
            dimension_semantics=("arbitrary",), vmem_limit_bytes=VMEM_LIMIT_BYTES),
        name="ffn_ln",
    )(x2d, w_gu, w_down, g, b)


def _split_bf16(x):
    hi = x.astype(BF16)
    lo = (x - hi.astype(F32)).astype(BF16)
    return hi, lo


def _block_diag(y_l, n_blocks):
    c = y_l.shape[0]
    lane_block = lax.broadcasted_iota(jnp.int32, y_l.shape, 1) // c
    zero = jnp.zeros_like(y_l)
    return jnp.concatenate([jnp.where(lane_block == b, y_l, zero) for b in range(n_blocks)], axis=0)


def _blockwise_matmul(l_l, y_l, n_blocks):
    lh, ll = _split_bf16(l_l)
    yh, yl = _split_bf16(y_l)
    m = l_l.shape[0]
    top = _dot(jnp.concatenate([lh, ll], axis=0), _block_diag(yh, n_blocks))
    return top[:m] + top[m:] + _dot(lh, _block_diag(yl, n_blocks))


def _unit_lower_inverses(a_ls, n_blocks):
    c = a_ls[0].shape[0]
    row = lax.broadcasted_iota(jnp.int32, a_ls[0].shape, 0)
    col = lax.broadcasted_iota(jnp.int32, a_ls[0].shape, 1) % c
    eye = (row == col).astype(F32)
    ps = [eye - a_l for a_l in a_ls]
    ys = [_blockwise_matmul(-a_l, -a_l, n_blocks) for a_l in a_ls]
    power = 2
    while power < c:
        if 2 * power < c:
            stacked = [_blockwise_matmul(jnp.concatenate([p, y], axis=0), y, n_blocks)
                       for p, y in zip(ps, ys)]
            ps = [p + s[:c] for p, s in zip(ps, stacked)]
            ys = [s[c:] for s in stacked]
        else:
            ps = [p + _blockwise_matmul(p, y, n_blocks) for p, y in zip(ps, ys)]
        power *= 2
    return ps


def _mixers_kernel(h_ref, w_ref, convw_ref, alog_ref,
                   dtb_ref, normw_ref, sinks_ref, odn_ref, osw_ref,
                   xbuf, q_s, k_s, kb_s, qd_s, kt_s, rhs_s, z_s, gcum_s, gcol_s, elast_s, qk_s, u_s, w_s, state, kvbuf, qlo_s, qhi_s,
                   *, tm):
    j = pl.program_id(1)
    n_qkv = 3 * DN_HEADS * DN_DK
    n_dn = DN_HEADS * DN_DK
    halo = SUBLANES
    c_ = DN_CHUNK
    n_chunks = tm // c_
    gate_lane = DN_HEADS

    @pl.when(j == 0)
    def _reset():
        xbuf[0:halo, :] = jnp.zeros((halo, n_qkv), F32)
        state[...] = jnp.zeros_like(state)
        kvbuf[0:WINDOW, :] = jnp.zeros((WINDOW, kvbuf.shape[1]), BF16)

    hb = h_ref[0].astype(BF16)

    pre = _dot(hb, w_ref[:, MIX_QKV])
    xbuf[halo:halo + tm, :] = pre
    cw = convw_ref[...]
    conv = cw[DN_CONV - 1:DN_CONV, :] * pre
    for s in range(1, DN_CONV):
        conv = conv + cw[DN_CONV - 1 - s:DN_CONV - s, :] * xbuf[halo - s:halo - s + tm, :]
    xbuf[0:halo, :] = xbuf[tm:tm + halo, :]
    qkv = _silu(conv)
    z_s[...] = _silu(_dot(hb, w_ref[:, MIX_Z]))

    ba = _dot(hb, w_ref[:, MIX_BA])
    beta = jax.nn.sigmoid(ba[:, :LANES])
    g = -jnp.exp(alog_ref[...]) * jax.nn.softplus(ba[:, LANES:] + dtb_ref[...])
    row_c = lax.broadcasted_iota(jnp.int32, (c_, c_), 0)
    col_c = lax.broadcasted_iota(jnp.int32, (c_, c_), 1)
    tril_f = (row_c >= col_c).astype(F32)
    gcum_parts, etail_parts, elast_parts = [], [], []
    for ci in range(n_chunks):
        gc = jnp.dot(tril_f, g[ci * c_:(ci + 1) * c_, :], precision=HIGHEST,
                     preferred_element_type=F32)
        glast = gc[c_ - 1:c_, :]
        gcum_parts.append(gc)
        etail_parts.append(jnp.exp(glast - gc))
        elast_parts.append(jnp.broadcast_to(jnp.exp(glast), (c_, LANES)))
    gcum = jnp.concatenate(gcum_parts, axis=0)
    gcum_s[...] = gcum
    elast_s[...] = jnp.concatenate(elast_parts, axis=0)
    sel_k = lax.broadcasted_iota(jnp.int32, (LANES, DN_HEADS * c_), 0)
    sel_n = lax.broadcasted_iota(jnp.int32, (LANES, DN_HEADS * c_), 1)
    gcol_s[...] = jnp.dot(gcum, (sel_k == gate_lane + sel_n // c_).astype(F32), precision=HIGHEST,
                          preferred_element_type=F32)
    bk = lax.broadcasted_iota(jnp.int32, (LANES, n_dn), 0)
    bn = lax.broadcasted_iota(jnp.int32, (LANES, n_dn), 1)
    spread = (bk == gate_lane + bn // DN_DK).astype(BF16)
    stacked = jnp.concatenate([beta, jnp.exp(gcum), jnp.concatenate(etail_parts, axis=0)], axis=0)
    bc = _dot(stacked.astype(BF16), spread)
    for hd in range(DN_HEADS):
        lo = hd * DN_DK
        hs = slice(lo, lo + DN_DK)
        qh = qkv[:, lo:lo + DN_DK]
        qh = qh * (lax.rsqrt(jnp.sum(qh * qh, -1, keepdims=True) + RMS_EPS) * (DN_DK ** -0.5))
        kh = qkv[:, n_dn + lo:n_dn + lo + DN_DK]
        kh = kh * lax.rsqrt(jnp.sum(kh * kh, -1, keepdims=True) + RMS_EPS)
        vh = qkv[:, 2 * n_dn + lo:2 * n_dn + lo + DN_DV]
        beta_b = bc[0:tm, hs]
        eg_b = bc[tm:2 * tm, hs]
        et_b = bc[2 * tm:3 * tm, hs]
        kb = kh * beta_b
        q_s[:, hs] = qh.astype(BF16)
        k_s[:, hs] = kh.astype(BF16)
        kb_s[:, hs] = kb.astype(BF16)
        qd_s[:, hs] = (qh * eg_b).astype(BF16)
        kt_s[:, hs] = (kh * et_b).astype(BF16)
        rhs_s[:, 2 * lo:2 * lo + DN_DV] = (vh * beta_b).astype(BF16)
        rhs_s[:, 2 * lo + DN_DV:2 * lo + DN_DV + DN_DK] = (kb * eg_b).astype(BF16)

    stack_w = DN_HEADS * c_
    row_l = lax.broadcasted_iota(jnp.int32, (c_, stack_w), 0)
    col_l = lax.broadcasted_iota(jnp.int32, (c_, stack_w), 1) % c_
    tril_l = row_l >= col_l
    strict_l = row_l > col_l
    lane = lax.broadcasted_iota(jnp.int32, (c_, LANES), 1)
    head_block = lax.broadcasted_iota(jnp.int32, (c_, n_dn), 1) // DN_DK
    ones_l = jnp.ones((c_, LANES), F32)
    normw = normw_ref[...]
    zero_rhs = jnp.zeros((c_, DN_DV + DN_DK), BF16)
    zero_v = jnp.zeros((c_, DN_DV), BF16)

    def wy_group_body(gi, carry):
        g0 = pl.multiple_of(gi * (WY_GROUP * c_), WY_GROUP * c_)
        chunk_rows = [pl.ds(g0 + ci * c_, c_) for ci in range(WY_GROUP)]
        decays = []
        for rows in chunk_rows:
            gc = gcum_s[rows, :]
            g_rows = jnp.concatenate(
                [jnp.where(lane == gate_lane + hd, gc, 0.0) for hd in range(DN_HEADS)], axis=0)
            grow_l = _dot_nt(ones_l, g_rows, precision=HIGHEST)
            decays.append(jnp.exp(jnp.where(tril_l, gcol_s[rows, :] - grow_l, NEG_INF)))
        kqs = []
        for rows in chunk_rows:
            k_rows = k_s[rows, :]
            k_bd = jnp.concatenate(
                [jnp.where(head_block == hd, k_rows, jnp.zeros_like(k_rows)) for hd in range(DN_HEADS)],
                axis=0)
            kqs.append(_dot_nt(jnp.concatenate([kb_s[rows, :], q_s[rows, :]], axis=0), k_bd))
        a_ls = [jnp.where(strict_l, kq[:c_] * dec, 0.0) for kq, dec in zip(kqs, decays)]
        for rows, kq, dec in zip(chunk_rows, kqs, decays):
            qk_s[rows, :] = jnp.where(tril_l, kq[c_:] * dec, 0.0).astype(BF16)
        t_invs = _unit_lower_inverses(a_ls, DN_HEADS)
        for rows, t_inv in zip(chunk_rows, t_invs):
            t_inv_b = t_inv.astype(BF16)
            for hd in range(DN_HEADS):
                lo = hd * DN_DK
                pair = slice((hd // 2) * LANES, (hd // 2 + 1) * LANES)
                rhs_h = rhs_s[rows, 2 * lo:2 * lo + DN_DV + DN_DK]
                rhs_pad = ([rhs_h, zero_rhs] if hd % 2 == 0 else [zero_rhs, rhs_h])
                sol = _dot(t_inv_b[:, pair], jnp.concatenate(rhs_pad, axis=0))
                u_s[rows, lo:lo + DN_DV] = sol[:, :DN_DV]
                w_s[rows, lo:lo + DN_DK] = sol[:, DN_DV:].astype(BF16)
        return carry

    def chunk_body(ci, carry):
        r0 = pl.multiple_of(ci * c_, c_)
        rows = pl.ds(r0, c_)
        e_last = elast_s[pl.ds(r0, 1), :]
        heads = range(DN_HEADS)
        hsl = [slice(hd * DN_DK, (hd + 1) * DN_DK) for hd in heads]
        pairs = [slice((hd // 2) * LANES, (hd // 2 + 1) * LANES) for hd in heads]
        s_old = [state[hd] for hd in heads]
        ws_qs = [_dot(jnp.concatenate([w_s[rows, hsl[hd]], qd_s[rows, hsl[hd]]], axis=0),
                      s_old[hd].astype(BF16)) for hd in heads]
        v_new_b = [(u_s[rows, hsl[hd]] - ws_qs[hd][:c_]).astype(BF16) for hd in heads]
        for hd in heads:
            state[hd] = (s_old[hd] * e_last[:, gate_lane + hd:gate_lane + hd + 1]
                         + _dot_tn(kt_s[rows, hsl[hd]], v_new_b[hd]))
        for hd in heads:
            v_pad = ([v_new_b[hd], zero_v] if hd % 2 == 0 else [zero_v, v_new_b[hd]])
            o = ws_qs[hd][c_:] + _dot(qk_s[rows, pairs[hd]], jnp.concatenate(v_pad, axis=0))
            o = o * lax.rsqrt(jnp.mean(o * o, -1, keepdims=True) + RMS_EPS) * normw
            o = o * z_s[rows, hsl[hd]]
            odn_ref[0, rows, hsl[hd]] = o.astype(odn_ref.dtype)
        return carry

    w_ = WINDOW
    group = SWA_HEADS // SWA_KV_HEADS
    swq = _dot(hb, w_ref[:, MIX_SWQ]) * (SWA_DH ** -0.5)
    kv = _dot(hb, w_ref[:, MIX_SWKV])
    lane_kv = lax.broadcasted_iota(jnp.int32, (tm, LANES), 1)
    first_half = lane_kv < SWA_DH
    for part in range(2):
        x = kv[:, part * LANES:(part + 1) * LANES]
        xr = pltpu.roll(x, SWA_DH, axis=1)
        dup0 = jnp.where(first_half, x, xr)
        dup1 = jnp.where(first_half, xr, x)
        kvbuf[w_:w_ + tm, (2 * part) * LANES:(2 * part + 1) * LANES] = dup0.astype(BF16)
        kvbuf[w_:w_ + tm, (2 * part + 1) * LANES:(2 * part + 2) * LANES] = dup1.astype(BF16)

    lane_q = lax.broadcasted_iota(jnp.int32, (tm, SWA_HEADS * SWA_DH), 1) % LANES
    qlo_s[...] = jnp.where(lane_q < SWA_DH, swq, 0.0).astype(BF16)
    qhi_s[...] = jnp.where(lane_q < SWA_DH, 0.0, swq).astype(BF16)

    qi = lax.broadcasted_iota(jnp.int32, (group * w_, 2 * w_), 0) % w_
    kj = lax.broadcasted_iota(jnp.int32, (group * w_, 2 * w_), 1)
    band = (kj > qi) & (kj <= qi + w_)
    lo_half = lax.broadcasted_iota(jnp.int32, (w_, LANES), 1) < SWA_DH
    sinks4 = [jnp.concatenate([jnp.full((w_, 1), sinks_ref[kvh * group + gi], F32)
                               for gi in range(group)], axis=0) for kvh in range(SWA_KV_HEADS)]
    kv_heads = range(SWA_KV_HEADS)

    ones_v = jnp.ones((2 * w_, LANES), BF16)

    problems = [(blk, kvh) for blk in range(tm // w_) for kvh in kv_heads]

    def swa_scores(blk, kvh):
        pieces = []
        for pair in range(group // 2):
            cols = slice((kvh * (group // 2) + pair) * LANES, (kvh * (group // 2) + pair + 1) * LANES)
            pieces += [qlo_s[blk * w_:(blk + 1) * w_, cols], qhi_s[blk * w_:(blk + 1) * w_, cols]]
        q4 = jnp.concatenate(pieces, axis=0)
        kd = kvbuf[blk * w_:(blk + 2) * w_, kvh * LANES:(kvh + 1) * LANES]
        mask = band & jnp.logical_or(kj >= w_, j > 0) if blk == 0 else band
        return jnp.where(mask, _dot_nt(q4, kd), NEG_INF)

    def swa_outputs():
        scores = [swa_scores(blk, kvh) for blk, kvh in problems]
        maxes = [jnp.maximum(jnp.max(s, axis=-1, keepdims=True), sinks4[kvh])
                 for s, (blk, kvh) in zip(scores, problems)]
        probs = [jnp.exp(s - m).astype(BF16) for s, m in zip(scores, maxes)]
        denoms = [_dot(p, ones_v) + jnp.exp(sinks4[kvh] - m)
                  for p, m, (blk, kvh) in zip(probs, maxes, problems)]
        for p, denom, (blk, kvh) in zip(probs, denoms, problems):
            vd = kvbuf[blk * w_:(blk + 2) * w_, (2 + kvh) * LANES:(3 + kvh) * LANES]
            o4 = _dot(p, vd) / denom
            for pair in range(group // 2):
                cols = slice((kvh * (group // 2) + pair) * LANES, (kvh * (group // 2) + pair + 1) * LANES)
                o_pair = jnp.where(lo_half, o4[(2 * pair) * w_:(2 * pair + 1) * w_],
                                   o4[(2 * pair + 1) * w_:(2 * pair + 2) * w_])
                osw_ref[0, blk * w_:(blk + 1) * w_, cols] = o_pair.astype(osw_ref.dtype)

    lax.fori_loop(0, n_chunks // WY_GROUP, wy_group_body, 0)
    lax.fori_loop(0, n_chunks, chunk_body, 0)
    swa_outputs()
    kvbuf[0:w_, :] = kvbuf[tm:tm + w_, :]


def _mixers(h, w_mix, layer, convw, alog_row, dtb_row, normw_row, sinks):
    b, s, d = h.shape
    tm = min(TOKEN_TILE, s)
    n_qkv = 3 * DN_HEADS * DN_DK
    dn_w = DN_HEADS * DN_DV
    sw_w = SWA_HEADS * SWA_DH
    full = lambda shape: pl.BlockSpec(shape, lambda bi, ji: tuple(0 for _ in shape))
    tok = lambda width: pl.BlockSpec((1, tm, width), lambda bi, ji: (bi, ji, 0))
    return pl.pallas_call(
        functools.partial(_mixers_kernel, tm=tm),
        out_shape=(jax.ShapeDtypeStruct((b, s, dn_w), BF16), jax.ShapeDtypeStruct((b, s, sw_w), BF16)),
        grid=(b, s // tm),
        in_specs=[
            tok(d),
            pl.BlockSpec((None, d, w_mix.shape[-1]), lambda bi, ji: (layer, 0, 0)),
            full(convw.shape), full(alog_row.shape), full(dtb_row.shape), full(normw_row.shape),
            pl.BlockSpec(memory_space=pltpu.SMEM),
        ],
        out_specs=(tok(dn_w), tok(sw_w)),
        scratch_shapes=[
            pltpu.VMEM((tm + SUBLANES, n_qkv), F32),
            pltpu.VMEM((tm, dn_w), BF16),
            pltpu.VMEM((tm, dn_w), BF16),
            pltpu.VMEM((tm, dn_w), BF16),
            pltpu.VMEM((tm, dn_w), BF16),
            pltpu.VMEM((tm, dn_w), BF16),
            pltpu.VMEM((tm, 2 * dn_w), BF16),
            pltpu.VMEM((tm, dn_w), F32),
            pltpu.VMEM((tm, LANES), F32),
            pltpu.VMEM((tm, DN_HEADS * DN_CHUNK), F32),
            pltpu.VMEM((tm, LANES), F32),
            pltpu.VMEM((tm, DN_HEADS * DN_CHUNK), BF16),
            pltpu.VMEM((tm, dn_w), F32),
            pltpu.VMEM((tm, dn_w), BF16),
            pltpu.VMEM((DN_HEADS, DN_DK, DN_DV), F32),
            pltpu.VMEM((tm + WINDOW, 4 * LANES), BF16),
            pltpu.VMEM((tm, sw_w), BF16),
            pltpu.VMEM((tm, sw_w), BF16),
        ],
        compiler_params=pltpu.CompilerParams(
            dimension_semantics=("arbitrary", "arbitrary"), vmem_limit_bytes=VMEM_LIMIT_BYTES),
        name="mixers",
    )(h, w_mix, convw, alog_row, dtb_row, normw_row, sinks)


def _merge_kernel(h_ref, odn_ref, osw_ref, mem_ref, memg_ref, memb_ref, wmkv_ref, w_ref,
                  wb_ref, wo_ref, g_ref, b_ref, o_ref, kmem, vmem, merged, *, alpha):
    j = pl.program_id(1)
    xa_w = XA_HEADS * XA_DH

    @pl.when(j == 0)
    def _memory_kv():
        mem_n = _layer_norm(mem_ref[0], memg_ref[...], memb_ref[...])
        kvm = _dot(mem_n.astype(BF16), wmkv_ref[...])
        kmem[...] = kvm[:, :xa_w].astype(BF16)
        vmem[...] = kvm[:, xa_w:].astype(BF16)

    hf = h_ref[0]
    hb = hf.astype(BF16)
    xq = _dot(hb, w_ref[:, MRG_XAQ]) * (XA_DH ** -0.5)
    hsl = [slice(hd * XA_DH, (hd + 1) * XA_DH) for hd in range(XA_HEADS)]
    xqb = xq.astype(BF16)
    scores = [_dot_nt(xqb[:, sl], kmem[:, sl]) for sl in hsl]
    probs = [jnp.exp(s - jnp.max(s, axis=-1, keepdims=True)) for s in scores]
    denoms = [jnp.sum(p, axis=-1, keepdims=True) for p in probs]
    heads = [_dot(p.astype(BF16), vmem[:, sl]) / dn for p, dn, sl in zip(probs, denoms, hsl)]
    oxa = jnp.concatenate(heads, axis=1).astype(BF16)
    branches = (odn_ref[0], osw_ref[0], oxa)
    d = hf.shape[-1]
    for dc in range(d // MERGE_CHUNK):
        lo = dc * MERGE_CHUNK
        acc = None
        for n in range(N_BRANCH):
            g0 = MRG_GATE0 + n * d + lo
            gate = jax.nn.sigmoid(_dot(hb, w_ref[:, g0:g0 + MERGE_CHUNK]))
            term = gate * _dot(branches[n], wb_ref[n, :, lo:lo + MERGE_CHUNK])
            acc = term if acc is None else acc + term
        merged[:, lo:lo + MERGE_CHUNK] = acc.astype(BF16)
    y = alpha * hf + _dot(merged[...], wo_ref[...])
    o_ref[0] = _layer_norm(y, g_ref[...], b_ref[...])


def _merge(h, odn, osw, mem, memg, memb, wmkv, w_mrg, wbranch, wout, layer, g, b_, alpha):
    b, s, d = h.shape
    tm = min(TOKEN_TILE, s)
    n_mem = mem.shape[1]
    xa_w = XA_HEADS * XA_DH
    full = lambda shape: pl.BlockSpec(shape, lambda bi, ji: tuple(0 for _ in shape))
    tok = lambda width: pl.BlockSpec((1, tm, width), lambda bi, ji: (bi, ji, 0))
    layer_block = lambda w: pl.BlockSpec((None,) + w.shape[1:],
                                         lambda bi, ji: (layer,) + (0,) * (w.ndim - 1))
    return pl.pallas_call(
        functools.partial(_merge_kernel, alpha=alpha),
        out_shape=jax.ShapeDtypeStruct((b, s, d), F32),
        grid=(b, s // tm),
        in_specs=[
            tok(d), tok(odn.shape[-1]), tok(osw.shape[-1]),
            pl.BlockSpec((1, n_mem, d), lambda bi, ji: (bi, 0, 0)),
            full(memg.shape), full(memb.shape), layer_block(wmkv), layer_block(w_mrg),
            layer_block(wbranch), layer_block(wout), full(g.shape), full(b_.shape),
        ],
        out_specs=tok(d),
        scratch_shapes=[
            pltpu.VMEM((n_mem, xa_w), BF16),
            pltpu.VMEM((n_mem, xa_w), BF16),
            pltpu.VMEM((tm, d), BF16),
        ],
        compiler_params=pltpu.CompilerParams(
            dimension_semantics=("arbitrary", "arbitrary"), vmem_limit_bytes=VMEM_LIMIT_BYTES),
        name="merge",
    )(h, odn, osw, mem, memg, memb, wmkv, w_mrg, wbranch, wout, g, b_)


def _lane_row(vals, offset):
    return jnp.zeros((1, LANES), F32).at[0, offset:offset + vals.shape[0]].set(vals.astype(F32))


def _pack_input_projection(w_in):
    depth, d, _ = w_in.shape
    o_b = 3 * _N_DN
    o_a = o_b + DN_HEADS
    o_z = o_a + DN_HEADS
    o_swq = o_z + DN_HEADS * DN_DV
    o_swk = o_swq + SWA_HEADS * SWA_DH
    o_xaq = o_swk + 2 * SWA_KV_HEADS * SWA_DH
    zeros = lambda n: jnp.zeros((depth, d, n), w_in.dtype)
    ba = [zeros(DN_HEADS), w_in[..., o_b:o_a], zeros(LANES - 2 * DN_HEADS),
          zeros(DN_HEADS), w_in[..., o_a:o_z], zeros(LANES - 2 * DN_HEADS)]
    w_mix = jnp.concatenate([w_in[..., :o_b], w_in[..., o_z:o_xaq]] + ba, axis=-1).astype(BF16)
    assert w_mix.shape[-1] == MIX_WIDTH
    w_mrg = w_in[..., o_xaq:].astype(BF16)
    return w_mix, w_mrg


def _mixer_tables(conv_w, a_log, dt_bias, norm_w, sinks):
    return (conv_w.astype(F32), _lane_row(a_log, DN_HEADS), _lane_row(dt_bias, DN_HEADS),
            norm_w.reshape(1, -1).astype(F32), sinks.astype(F32))


def kernel(x, mem, mem_ln_g, mem_ln_b, ln_g, ln_b, ffn1_w_gu, ffn1_w_down, w_in, dn_conv_w, dn_a_log,
           dn_dt_bias, dn_norm_w, swa_sinks, w_mem_kv, w_branch, w_out, ffn2_w_gu, ffn2_w_down):
    b, s, d = x.shape
    depth = ln_g.shape[0]
    alpha = float((2 * depth) ** 0.25)
    row = lambda v: v.reshape(1, -1).astype(F32)

    w_mix, w_mrg = _pack_input_projection(w_in)
    ffn1_gu, ffn1_down = ffn1_w_gu.astype(BF16), ffn1_w_down.astype(BF16)
    ffn2_gu, ffn2_down = ffn2_w_gu.astype(BF16), ffn2_w_down.astype(BF16)
    w_mem_kv_b, w_branch_b, w_out_b = w_mem_kv.astype(BF16), w_branch.astype(BF16), w_out.astype(BF16)

    for l in range(depth):
        h = _ffn_ln(x.reshape(b * s, d), ffn1_gu, ffn1_down, l, row(ln_g[l, 0]), row(ln_b[l, 0]), alpha)
        h = h.reshape(b, s, d)
        odn, osw = _mixers(h, w_mix, l, *_mixer_tables(dn_conv_w[l], dn_a_log[l], dn_dt_bias[l],
                                                       dn_norm_w[l], swa_sinks[l]))
        h = _merge(h, odn, osw, mem, row(mem_ln_g), row(mem_ln_b), w_mem_kv_b, w_mrg, w_branch_b,
                   w_out_b, l, row(ln_g[l, 1]), row(ln_b[l, 1]), alpha)
        x = _ffn_ln(h.reshape(b * s, d), ffn2_gu, ffn2_down, l, row(ln_g[l, 2]), row(ln_b[l, 2]), alpha)
        x = x.reshape(b, s, d)
    return x
```

```python
import functools

import jax
import jax.numpy as jnp
from jax import lax
from jax.experimental import pallas as pl
from jax.experimental.pallas import tpu as pltpu

F32 = jnp.float32
BF16 = jnp.bfloat16

DN_HEADS, DN_DK, DN_DV, DN_CONV, DN_CHUNK = 4, 128, 128, 4, 64
SWA_HEADS, SWA_KV_HEADS, SWA_DH, WINDOW = 8, 2, 64, 128
XA_HEADS, XA_DH = 4, 128
N_BRANCH, BRANCH_W = 3, 512
LN_EPS, RMS_EPS, NEG_INF = 1e-5, 1e-6, -1e30

LANES = 128
SUBLANES = 8
VMEM_LIMIT_BYTES = 56 * 1024 * 1024

TOKEN_TILE = 512
FFN_TOKEN_TILE = 1024
FF_CHUNK = 256
MERGE_CHUNK = 256
WY_GROUP = 8

_N_DN = DN_HEADS * DN_DK
MIX_QKV = slice(0, 3 * _N_DN)
MIX_Z = slice(MIX_QKV.stop, MIX_QKV.stop + DN_HEADS * DN_DV)
MIX_SWQ = slice(MIX_Z.stop, MIX_Z.stop + SWA_HEADS * SWA_DH)
MIX_SWKV = slice(MIX_SWQ.stop, MIX_SWQ.stop + 2 * SWA_KV_HEADS * SWA_DH)
MIX_BA = slice(MIX_SWKV.stop, MIX_SWKV.stop + 2 * LANES)
MIX_WIDTH = MIX_BA.stop
MRG_XAQ = slice(0, XA_HEADS * XA_DH)
MRG_GATE0 = MRG_XAQ.stop


def _dot(a, b):
    return jnp.dot(a, b, preferred_element_type=F32)


def _dot_nt(a, b):
    return lax.dot_general(a, b, (((1,), (1,)), ((), ())), preferred_element_type=F32)


def _dot_tn(a, b):
    return lax.dot_general(a, b, (((0,), (0,)), ((), ())), preferred_element_type=F32)


def _layer_norm(y, g, b):
    mu = jnp.mean(y, axis=-1, keepdims=True)
    d = y - mu
    var = jnp.mean(d * d, axis=-1, keepdims=True)
    return d * lax.rsqrt(var + LN_EPS) * g + b


def _silu(x):
    return x * jax.nn.sigmoid(x)


def _ffn_ln_kernel(x_ref, wgu_ref, wd_ref, g_ref, b_ref, o_ref, acc_ref, *, alpha):
    d_ff = wd_ref.shape[0]
    xf = x_ref[...]
    xb = xf.astype(BF16)
    for c in range(d_ff // FF_CHUNK):
        lo = c * FF_CHUNK
        gate = _dot(xb, wgu_ref[:, lo:lo + FF_CHUNK])
        up = _dot(xb, wgu_ref[:, d_ff + lo:d_ff + lo + FF_CHUNK])
        act = (_silu(gate) * up).astype(BF16)
        part = _dot(act, wd_ref[lo:lo + FF_CHUNK, :])
        if c == 0:
            acc_ref[...] = part
        else:
            acc_ref[...] += part
    y = alpha * xf + 0.5 * acc_ref[...]
    o_ref[...] = _layer_norm(y, g_ref[...], b_ref[...])


def _ffn_ln(x2d, w_gu, w_down, layer, g, b, alpha):
    t, d = x2d.shape
    _, d_ff, _ = w_down.shape
    tm = min(FFN_TOKEN_TILE, t)
    return pl.pallas_call(
        functools.partial(_ffn_ln_kernel, alpha=alpha),
        out_shape=jax.ShapeDtypeStruct((t, d), F32),
        grid=(t // tm,),
        in_specs=[
            pl.BlockSpec((tm, d), lambda i: (i, 0)),
            pl.BlockSpec((None, d, 2 * d_ff), lambda i: (layer, 0, 0), pipeline_mode=pl.Buffered(1)),
            pl.BlockSpec((None, d_ff, d), lambda i: (layer, 0, 0), pipeline_mode=pl.Buffered(1)),
            pl.BlockSpec((1, d), lambda i: (0, 0)),
            pl.BlockSpec((1, d), lambda i: (0, 0)),
        ],
        out_specs=pl.BlockSpec((tm, d), lambda i: (i, 0)),
        scratch_shapes=[pltpu.VMEM((tm, d), F32)],
        compiler_params=pltpu.CompilerParams(
            dimension_semantics=("arbitrary",), vmem_limit_bytes=VMEM_LIMIT_BYTES),
        name="ffn_ln",
    )(x2d, w_gu, w_down, g, b)


def _split_bf16(x):
    hi = x.astype(BF16)
    lo = (x - hi.astype(F32)).astype(BF16)
    return hi, lo


def _split3_bf16(x):
    p1 = x.astype(BF16)
    r1 = x - p1.astype(F32)
    p2 = r1.astype(BF16)
    p3 = (r1 - p2.astype(F32)).astype(BF16)
    return p1, p2, p3


def _block_diag(y_l, n_blocks):
    c = y_l.shape[0]
    lane_block = lax.broadcasted_iota(jnp.int32, y_l.shape, 1) // c
    zero = jnp.zeros_like(y_l)
    return jnp.concatenate([jnp.where(lane_block == b, y_l, zero) for b in range(n_blocks)], axis=0)


def _blockwise_matmul(l_l, y_l, n_blocks):
    lh, ll = _split_bf16(l_l)
    yh, yl = _split_bf16(y_l)
    m = l_l.shape[0]
    top = _dot(jnp.concatenate([lh, ll], axis=0), _block_diag(yh, n_blocks))
    return top[:m] + top[m:] + _dot(lh, _block_diag(yl, n_blocks))


def _unit_lower_inverses(a_ls, n_blocks):
    c = a_ls[0].shape[0]
    row = lax.broadcasted_iota(jnp.int32, a_ls[0].shape, 0)
    col = lax.broadcasted_iota(jnp.int32, a_ls[0].shape, 1) % c
    eye = (row == col).astype(F32)
    ps = [eye - a_l for a_l in a_ls]
    ys = [_blockwise_matmul(-a_l, -a_l, n_blocks) for a_l in a_ls]
    power = 2
    while power < c:
        if 2 * power < c:
            stacked = [_blockwise_matmul(jnp.concatenate([p, y], axis=0), y, n_blocks)
                       for p, y in zip(ps, ys)]
            ps = [p + s[:c] for p, s in zip(ps, stacked)]
            ys = [s[c:] for s in stacked]
        else:
            ps = [p + _blockwise_matmul(p, y, n_blocks) for p, y in zip(ps, ys)]
        power *= 2
    return ps


def _mixers_kernel(h_ref, w_ref, convw_ref, alog_ref,
                   dtb_ref, normw_ref, sinks_ref, odn_ref, osw_ref,
                   xbuf, q_s, k_s, kb_s, qd_s, kt_s, rhs_s, z_s, gcum_s, gcol_s, elast_s, o0_s, pq_s, n_s, state, kvbuf, qlo_s, qhi_s,
                   *, tm):
    j = pl.program_id(1)
    n_qkv = 3 * DN_HEADS * DN_DK
    n_dn = DN_HEADS * DN_DK
    halo = SUBLANES
    c_ = DN_CHUNK
    n_chunks = tm // c_
    gate_lane = DN_HEADS

    @pl.when(j == 0)
    def _reset():
        xbuf[0:halo, :] = jnp.zeros((halo, n_qkv), F32)
        state[...] = jnp.zeros_like(state)
        kvbuf[0:WINDOW, :] = jnp.zeros((WINDOW, kvbuf.shape[1]), BF16)

    hb = h_ref[0].astype(BF16)

    pre = _dot(hb, w_ref[:, MIX_QKV])
    xbuf[halo:halo + tm, :] = pre
    cw = convw_ref[...]
    conv = cw[DN_CONV - 1:DN_CONV, :] * pre
    for s in range(1, DN_CONV):
        conv = conv + cw[DN_CONV - 1 - s:DN_CONV - s, :] * xbuf[halo - s:halo - s + tm, :]
    xbuf[0:halo, :] = xbuf[tm:tm + halo, :]
    qkv = _silu(conv)
    z_s[...] = _silu(_dot(hb, w_ref[:, MIX_Z]))

    ba = _dot(hb, w_ref[:, MIX_BA])
    beta = jax.nn.sigmoid(ba[:, :LANES])
    g = -jnp.exp(alog_ref[...]) * jax.nn.softplus(ba[:, LANES:] + dtb_ref[...])
    row_c = lax.broadcasted_iota(jnp.int32, (c_, c_), 0)
    col_c = lax.broadcasted_iota(jnp.int32, (c_, c_), 1)
    tril_b = (row_c >= col_c).astype(BF16)
    g_parts = _split3_bf16(g)
    gcum_parts, etail_parts, elast_parts = [], [], []
    for ci in range(n_chunks):
        gc = sum(_dot(tril_b, part[ci * c_:(ci + 1) * c_, :]) for part in g_parts)
        glast = gc[c_ - 1:c_, :]
        gcum_parts.append(gc)
        etail_parts.append(jnp.exp(glast - gc))
        elast_parts.append(jnp.broadcast_to(jnp.exp(glast), (c_, LANES)))
    gcum = jnp.concatenate(gcum_parts, axis=0)
    gcum_s[...] = gcum
    elast_s[...] = jnp.concatenate(elast_parts, axis=0)
    sel_k = lax.broadcasted_iota(jnp.int32, (LANES, DN_HEADS * c_), 0)
    sel_n = lax.broadcasted_iota(jnp.int32, (LANES, DN_HEADS * c_), 1)
    sel_b = (sel_k == gate_lane + sel_n // c_).astype(BF16)
    gcol_s[...] = sum(_dot(part, sel_b) for part in _split3_bf16(gcum))
    bk = lax.broadcasted_iota(jnp.int32, (LANES, n_dn), 0)
    bn = lax.broadcasted_iota(jnp.int32, (LANES, n_dn), 1)
    spread = (bk == gate_lane + bn // DN_DK).astype(BF16)
    stacked = jnp.concatenate([beta, jnp.exp(gcum), jnp.concatenate(etail_parts, axis=0)], axis=0)
    bc = _dot(stacked.astype(BF16), spread)
    for hd in range(DN_HEADS):
        lo = hd * DN_DK
        hs = slice(lo, lo + DN_DK)
        qh = qkv[:, lo:lo + DN_DK]
        qh = qh * (lax.rsqrt(jnp.sum(qh * qh, -1, keepdims=True) + RMS_EPS) * (DN_DK ** -0.5))
        kh = qkv[:, n_dn + lo:n_dn + lo + DN_DK]
        kh = kh * lax.rsqrt(jnp.sum(kh * kh, -1, keepdims=True) + RMS_EPS)
        vh = qkv[:, 2 * n_dn + lo:2 * n_dn + lo + DN_DV]
        beta_b = bc[0:tm, hs]
        eg_b = bc[tm:2 * tm, hs]
        et_b = bc[2 * tm:3 * tm, hs]
        kb = kh * beta_b
        q_s[:, hs] = qh.astype(BF16)
        k_s[:, hs] = kh.astype(BF16)
        kb_s[:, hs] = kb.astype(BF16)
        qd_s[:, hs] = (qh * eg_b).astype(BF16)
        kt_s[:, hs] = (kh * et_b).astype(BF16)
        rhs_s[:, 2 * lo:2 * lo + DN_DV] = (vh * beta_b).astype(BF16)
        rhs_s[:, 2 * lo + DN_DV:2 * lo + DN_DV + DN_DK] = (kb * eg_b).astype(BF16)

    stack_w = DN_HEADS * c_
    row_l = lax.broadcasted_iota(jnp.int32, (c_, stack_w), 0)
    col_l = lax.broadcasted_iota(jnp.int32, (c_, stack_w), 1) % c_
    tril_l = row_l >= col_l
    strict_l = row_l > col_l
    lane = lax.broadcasted_iota(jnp.int32, (c_, LANES), 1)
    head_block = lax.broadcasted_iota(jnp.int32, (c_, n_dn), 1) // DN_DK
    ones_l = jnp.ones((c_, LANES), BF16)
    normw = normw_ref[...]
    zero_rhs = jnp.zeros((c_, DN_DV + DN_DK), BF16)
    heads = range(DN_HEADS)
    hsl = [slice(hd * DN_DK, (hd + 1) * DN_DK) for hd in heads]
    pairs = [slice((hd // 2) * LANES, (hd // 2 + 1) * LANES) for hd in heads]

    def wy_group_body(gi, carry):
        g0 = pl.multiple_of(gi * (WY_GROUP * c_), WY_GROUP * c_)
        chunk_rows = [pl.ds(g0 + ci * c_, c_) for ci in range(WY_GROUP)]
        decays = []
        for rows in chunk_rows:
            gc = gcum_s[rows, :]
            g_rows = jnp.concatenate(
                [jnp.where(lane == gate_lane + hd, gc, 0.0) for hd in range(DN_HEADS)], axis=0)
            grow_l = sum(_dot_nt(ones_l, part) for part in _split3_bf16(g_rows))
            decays.append(jnp.exp(jnp.where(tril_l, gcol_s[rows, :] - grow_l, NEG_INF)))
        kqs = []
        for rows in chunk_rows:
            k_rows = k_s[rows, :]
            k_bd = jnp.concatenate(
                [jnp.where(head_block == hd, k_rows, jnp.zeros_like(k_rows)) for hd in range(DN_HEADS)],
                axis=0)
            kqs.append(_dot_nt(jnp.concatenate([kb_s[rows, :], q_s[rows, :]], axis=0), k_bd))
        a_ls = [jnp.where(strict_l, kq[:c_] * dec, 0.0) for kq, dec in zip(kqs, decays)]
        qk_bs = [jnp.where(tril_l, kq[c_:] * dec, 0.0).astype(BF16) for kq, dec in zip(kqs, decays)]
        t_invs = _unit_lower_inverses(a_ls, DN_HEADS)
        sols = []
        for rows, t_inv in zip(chunk_rows, t_invs):
            t_inv_b = t_inv.astype(BF16)
            for hd in heads:
                rhs_h = rhs_s[rows, 2 * hd * DN_DK:2 * hd * DN_DK + DN_DV + DN_DK]
                rhs_pad = ([rhs_h, zero_rhs] if hd % 2 == 0 else [zero_rhs, rhs_h])
                sols.append(_dot(t_inv_b[:, pairs[hd]], jnp.concatenate(rhs_pad, axis=0)).astype(BF16))
        for n, (rows, qk_b) in enumerate(zip(chunk_rows, qk_bs)):
            for hd in heads:
                sol_b = sols[n * DN_HEADS + hd]
                sol_pad = ([sol_b, zero_rhs] if hd % 2 == 0 else [zero_rhs, sol_b])
                qk_uw = _dot(qk_b[:, pairs[hd]], jnp.concatenate(sol_pad, axis=0))
                kt_uw = _dot_tn(kt_s[rows, hsl[hd]], sol_b)
                q_eff = qd_s[rows, hsl[hd]].astype(F32) - qk_uw[:, DN_DV:]
                slot = (gi * WY_GROUP + n) * DN_HEADS + hd
                o0_s[rows, hsl[hd]] = qk_uw[:, :DN_DV]
                pq_s[slot] = jnp.concatenate([kt_uw[:, DN_DV:], q_eff], axis=0).astype(BF16)
                n_s[slot] = kt_uw[:, :DN_DV]
        return carry

    def chunk_body(ci, carry):
        r0 = pl.multiple_of(ci * c_, c_)
        rows = pl.ds(r0, c_)
        e_last = elast_s[pl.ds(r0, 1), :]
        s_old = [state[hd] for hd in heads]
        prods = [_dot(pq_s[ci * DN_HEADS + hd], s_old[hd].astype(BF16)) for hd in heads]
        for hd in heads:
            state[hd] = (s_old[hd] * e_last[:, gate_lane + hd:gate_lane + hd + 1]
                         - prods[hd][:DN_DK] + n_s[ci * DN_HEADS + hd])
        for hd in heads:
            o = prods[hd][DN_DK:] + o0_s[rows, hsl[hd]]
            o = o * lax.rsqrt(jnp.mean(o * o, -1, keepdims=True) + RMS_EPS) * normw
            o = o * z_s[rows, hsl[hd]]
            odn_ref[0, rows, hsl[hd]] = o.astype(odn_ref.dtype)
        return carry

    w_ = WINDOW
    group = SWA_HEADS // SWA_KV_HEADS
    swq = _dot(hb, w_ref[:, MIX_SWQ]) * (SWA_DH ** -0.5)
    kv = _dot(hb, w_ref[:, MIX_SWKV])
    lane_kv = lax.broadcasted_iota(jnp.int32, (tm, LANES), 1)
    first_half = lane_kv < SWA_DH
    for part in range(2):
        x = kv[:, part * LANES:(part + 1) * LANES]
        xr = pltpu.roll(x, SWA_DH, axis=1)
        dup0 = jnp.where(first_half, x, xr)
        dup1 = jnp.where(first_half, xr, x)
        kvbuf[w_:w_ + tm, (2 * part) * LANES:(2 * part + 1) * LANES] = dup0.astype(BF16)
        kvbuf[w_:w_ + tm, (2 * part + 1) * LANES:(2 * part + 2) * LANES] = dup1.astype(BF16)

    lane_q = lax.broadcasted_iota(jnp.int32, (tm, SWA_HEADS * SWA_DH), 1) % LANES
    qlo_s[...] = jnp.where(lane_q < SWA_DH, swq, 0.0).astype(BF16)
    qhi_s[...] = jnp.where(lane_q < SWA_DH, 0.0, swq).astype(BF16)

    qi = lax.broadcasted_iota(jnp.int32, (group * w_, 2 * w_), 0) % w_
    kj = lax.broadcasted_iota(jnp.int32, (group * w_, 2 * w_), 1)
    band = (kj > qi) & (kj <= qi + w_)
    lo_half = lax.broadcasted_iota(jnp.int32, (w_, LANES), 1) < SWA_DH
    sinks4 = [jnp.concatenate([jnp.full((w_, 1), sinks_ref[kvh * group + gi], F32)
                               for gi in range(group)], axis=0) for kvh in range(SWA_KV_HEADS)]
    kv_heads = range(SWA_KV_HEADS)

    ones_v = jnp.ones((2 * w_, LANES), BF16)

    problems = [(blk, kvh) for blk in range(tm // w_) for kvh in kv_heads]

    def swa_scores(blk, kvh):
        pieces = []
        for pair in range(group // 2):
            cols = slice((kvh * (group // 2) + pair) * LANES, (kvh * (group // 2) + pair + 1) * LANES)
            pieces += [qlo_s[blk * w_:(blk + 1) * w_, cols], qhi_s[blk * w_:(blk + 1) * w_, cols]]
        q4 = jnp.concatenate(pieces, axis=0)
        kd = kvbuf[blk * w_:(blk + 2) * w_, kvh * LANES:(kvh + 1) * LANES]
        mask = band & jnp.logical_or(kj >= w_, j > 0) if blk == 0 else band
        return jnp.where(mask, _dot_nt(q4, kd), NEG_INF)

    def swa_outputs():
        scores = [swa_scores(blk, kvh) for blk, kvh in problems]
        maxes = [jnp.maximum(jnp.max(s, axis=-1, keepdims=True), sinks4[kvh])
                 for s, (blk, kvh) in zip(scores, problems)]
        probs = [jnp.exp(s - m).astype(BF16) for s, m in zip(scores, maxes)]
        for p, m, (blk, kvh) in zip(probs, maxes, problems):
            vd = kvbuf[blk * w_:(blk + 2) * w_, (2 + kvh) * LANES:(3 + kvh) * LANES]
            pv = _dot(p, jnp.concatenate([vd, ones_v], axis=1))
            o4 = pv[:, :LANES] / (pv[:, LANES:] + jnp.exp(sinks4[kvh] - m))
            for pair in range(group // 2):
                cols = slice((kvh * (group // 2) + pair) * LANES, (kvh * (group // 2) + pair + 1) * LANES)
                o_pair = jnp.where(lo_half, o4[(2 * pair) * w_:(2 * pair + 1) * w_],
                                   o4[(2 * pair + 1) * w_:(2 * pair + 2) * w_])
                osw_ref[0, blk * w_:(blk + 1) * w_, cols] = o_pair.astype(osw_ref.dtype)

    lax.fori_loop(0, n_chunks // WY_GROUP, wy_group_body, 0)
    lax.fori_loop(0, n_chunks, chunk_body, 0)
    swa_outputs()
    kvbuf[0:w_, :] = kvbuf[tm:tm + w_, :]


def _mixers(h, w_mix, layer, convw, alog_row, dtb_row, normw_row, sinks):
    b, s, d = h.shape
    tm = min(TOKEN_TILE, s)
    n_qkv = 3 * DN_HEADS * DN_DK
    dn_w = DN_HEADS * DN_DV
    sw_w = SWA_HEADS * SWA_DH
    n_slots = (tm // DN_CHUNK) * DN_HEADS
    full = lambda shape: pl.BlockSpec(shape, lambda bi, ji: tuple(0 for _ in shape))
    tok = lambda width: pl.BlockSpec((1, tm, width), lambda bi, ji: (bi, ji, 0))
    return pl.pallas_call(
        functools.partial(_mixers_kernel, tm=tm),
        out_shape=(jax.ShapeDtypeStruct((b, s, dn_w), BF16), jax.ShapeDtypeStruct((b, s, sw_w), BF16)),
        grid=(b, s // tm),
        in_specs=[
            tok(d),
            pl.BlockSpec((None, d, w_mix.shape[-1]), lambda bi, ji: (layer, 0, 0)),
            full(convw.shape), full(alog_row.shape), full(dtb_row.shape), full(normw_row.shape),
            pl.BlockSpec(memory_space=pltpu.SMEM),
        ],
        out_specs=(tok(dn_w), tok(sw_w)),
        scratch_shapes=[
            pltpu.VMEM((tm + SUBLANES, n_qkv), F32),
            pltpu.VMEM((tm, dn_w), BF16),
            pltpu.VMEM((tm, dn_w), BF16),
            pltpu.VMEM((tm, dn_w), BF16),
            pltpu.VMEM((tm, dn_w), BF16),
            pltpu.VMEM((tm, dn_w), BF16),
            pltpu.VMEM((tm, 2 * dn_w), BF16),
            pltpu.VMEM((tm, dn_w), F32),
            pltpu.VMEM((tm, LANES), F32),
            pltpu.VMEM((tm, DN_HEADS * DN_CHUNK), F32),
            pltpu.VMEM((tm, LANES), F32),
            pltpu.VMEM((tm, dn_w), F32),
            pltpu.VMEM((n_slots, DN_DK + DN_CHUNK, DN_DK), BF16),
            pltpu.VMEM((n_slots, DN_DK, DN_DV), F32),
            pltpu.VMEM((DN_HEADS, DN_DK, DN_DV), F32),
            pltpu.VMEM((tm + WINDOW, 4 * LANES), BF16),
            pltpu.VMEM((tm, sw_w), BF16),
            pltpu.VMEM((tm, sw_w), BF16),
        ],
        compiler_params=pltpu.CompilerParams(
            dimension_semantics=("arbitrary", "arbitrary"), vmem_limit_bytes=VMEM_LIMIT_BYTES),
        name="mixers",
    )(h, w_mix, convw, alog_row, dtb_row, normw_row, sinks)


def _merge_kernel(h_ref, odn_ref, osw_ref, mem_ref, memg_ref, memb_ref, wmkv_ref, w_ref,
                  wb_ref, wo_ref, g_ref, b_ref, o_ref, kmem, vmem, merged, *, alpha):
    j = pl.program_id(1)
    xa_w = XA_HEADS * XA_DH

    @pl.when(j == 0)
    def _memory_kv():
        mem_n = _layer_norm(mem_ref[0], memg_ref[...], memb_ref[...])
        kvm = _dot(mem_n.astype(BF16), wmkv_ref[...])
        kmem[...] = kvm[:, :xa_w].astype(BF16)
        vmem[...] = kvm[:, xa_w:].astype(BF16)

    hf = h_ref[0]
    hb = hf.astype(BF16)
    xq = _dot(hb, w_ref[:, MRG_XAQ]) * (XA_DH ** -0.5)
    hsl = [slice(hd * XA_DH, (hd + 1) * XA_DH) for hd in range(XA_HEADS)]
    xqb = xq.astype(BF16)
    scores = [_dot_nt(xqb[:, sl], kmem[:, sl]) for sl in hsl]
    probs = [jnp.exp(s - jnp.max(s, axis=-1, keepdims=True)) for s in scores]
    denoms = [jnp.sum(p, axis=-1, keepdims=True) for p in probs]
    heads = [_dot(p.astype(BF16), vmem[:, sl]) / dn for p, dn, sl in zip(probs, denoms, hsl)]
    oxa = jnp.concatenate(heads, axis=1).astype(BF16)
    branches = (odn_ref[0], osw_ref[0], oxa)
    d = hf.shape[-1]
    for dc in range(d // MERGE_CHUNK):
        lo = dc * MERGE_CHUNK
        acc = None
        for n in range(N_BRANCH):
            g0 = MRG_GATE0 + n * d + lo
            gate = jax.nn.sigmoid(_dot(hb, w_ref[:, g0:g0 + MERGE_CHUNK]))
            term = gate * _dot(branches[n], wb_ref[n, :, lo:lo + MERGE_CHUNK])
            acc = term if acc is None else acc + term
        merged[:, lo:lo + MERGE_CHUNK] = acc.astype(BF16)
    y = alpha * hf + _dot(merged[...], wo_ref[...])
    o_ref[0] = _layer_norm(y, g_ref[...], b_ref[...])


def _merge(h, odn, osw, mem, memg, memb, wmkv, w_mrg, wbranch, wout, layer, g, b_, alpha):
    b, s, d = h.shape
    tm = min(TOKEN_TILE, s)
    n_mem = mem.shape[1]
    xa_w = XA_HEADS * XA_DH
    full = lambda shape: pl.BlockSpec(shape, lambda bi, ji: tuple(0 for _ in shape))
    tok = lambda width: pl.BlockSpec((1, tm, width), lambda bi, ji: (bi, ji, 0))
    layer_block = lambda w: pl.BlockSpec((None,) + w.shape[1:],
                                         lambda bi, ji: (layer,) + (0,) * (w.ndim - 1))
    return pl.pallas_call(
        functools.partial(_merge_kernel, alpha=alpha),
        out_shape=jax.ShapeDtypeStruct((b, s, d), F32),
        grid=(b, s // tm),
        in_specs=[
            tok(d), tok(odn.shape[-1]), tok(osw.shape[-1]),
            pl.BlockSpec((1, n_mem, d), lambda bi, ji: (bi, 0, 0)),
            full(memg.shape), full(memb.shape), layer_block(wmkv), layer_block(w_mrg),
            layer_block(wbranch), layer_block(wout), full(g.shape), full(b_.shape),
        ],
        out_specs=tok(d),
        scratch_shapes=[
            pltpu.VMEM((n_mem, xa_w), BF16),
            pltpu.VMEM((n_mem, xa_w), BF16),
            pltpu.VMEM((tm, d), BF16),
        ],
        compiler_params=pltpu.CompilerParams(
            dimension_semantics=("arbitrary", "arbitrary"), vmem_limit_bytes=VMEM_LIMIT_BYTES),
        name="merge",
    )(h, odn, osw, mem, memg, memb, wmkv, w_mrg, wbranch, wout, g, b_)


def _lane_row(vals, offset):
    return jnp.zeros((1, LANES), F32).at[0, offset:offset + vals.shape[0]].set(vals.astype(F32))


def _pack_input_projection(w_in):
    depth, d, _ = w_in.shape
    o_b = 3 * _N_DN
    o_a = o_b + DN_HEADS
    o_z = o_a + DN_HEADS
    o_swq = o_z + DN_HEADS * DN_DV
    o_swk = o_swq + SWA_HEADS * SWA_DH
    o_xaq = o_swk + 2 * SWA_KV_HEADS * SWA_DH
    zeros = lambda n: jnp.zeros((depth, d, n), w_in.dtype)
    ba = [zeros(DN_HEADS), w_in[..., o_b:o_a], zeros(LANES - 2 * DN_HEADS),
          zeros(DN_HEADS), w_in[..., o_a:o_z], zeros(LANES - 2 * DN_HEADS)]
    w_mix = jnp.concatenate([w_in[..., :o_b], w_in[..., o_z:o_xaq]] + ba, axis=-1).astype(BF16)
    assert w_mix.shape[-1] == MIX_WIDTH
    w_mrg = w_in[..., o_xaq:].astype(BF16)
    return w_mix, w_mrg


def _mixer_tables(conv_w, a_log, dt_bias, norm_w, sinks):
    return (conv_w.astype(F32), _lane_row(a_log, DN_HEADS), _lane_row(dt_bias, DN_HEADS),
            norm_w.reshape(1, -1).astype(F32), sinks.astype(F32))


def kernel(x, mem, mem_ln_g, mem_ln_b, ln_g, ln_b, ffn1_w_gu, ffn1_w_down, w_in, dn_conv_w, dn_a_log,
           dn_dt_bias, dn_norm_w, swa_sinks, w_mem_kv, w_branch, w_out, ffn2_w_gu, ffn2_w_down):
    b, s, d = x.shape
    depth = ln_g.shape[0]
    alpha = float((2 * depth) ** 0.25)
    row = lambda v: v.reshape(1, -1).astype(F32)

    w_mix, w_mrg = _pack_input_projection(w_in)
    ffn1_gu, ffn1_down = ffn1_w_gu.astype(BF16), ffn1_w_down.astype(BF16)
    ffn2_gu, ffn2_down = ffn2_w_gu.astype(BF16), ffn2_w_down.astype(BF16)
    w_mem_kv_b, w_branch_b, w_out_b = w_mem_kv.astype(BF16), w_branch.astype(BF16), w_out.astype(BF16)

    for l in range(depth):
        h = _ffn_ln(x.reshape(b * s, d), ffn1_gu, ffn1_down, l, row(ln_g[l, 0]), row(ln_b[l, 0]), alpha)
        h = h.reshape(b, s, d)
        odn, osw = _mixers(h, w_mix, l, *_mixer_tables(dn_conv_w[l], dn_a_log[l], dn_dt_bias[l],
                                                       dn_norm_w[l], swa_sinks[l]))
        h = _merge(h, odn, osw, mem, row(mem_ln_g), row(mem_ln_b), w_mem_kv_b, w_mrg, w_branch_b,
                   w_out_b, l, row(ln_g[l, 1]), row(ln_b[l, 1]), alpha)
        x = _ffn_ln(h.reshape(b * s, d), ffn2_gu, ffn2_down, l, row(ln_g[l, 2]), row(ln_b[l, 2]), alpha)
        x = x.reshape(b, s, d)
    return x
```

```python
import functools

import jax
import jax.numpy as jnp
from jax import lax
from jax.experimental import pallas as pl
from jax.experimental.pallas import tpu as pltpu

F32 = jnp.float32
BF16 = jnp.bfloat16

DN_HEADS, DN_DK, DN_DV, DN_CONV, DN_CHUNK = 4, 128, 128, 4, 64
SWA_HEADS, SWA_KV_HEADS, SWA_DH, WINDOW = 8, 2, 64, 128
XA_HEADS, XA_DH = 4, 128
N_BRANCH, BRANCH_W = 3, 512
LN_EPS, RMS_EPS, NEG_INF = 1e-5, 1e-6, -1e30

LANES = 128
SUBLANES = 8
VMEM_LIMIT_BYTES = 56 * 1024 * 1024

TOKEN_TILE = 512
FFN_TOKEN_TILE = 1024
FF_CHUNK = 256
MERGE_CHUNK = 256
WY_GROUP = 8

_N_DN = DN_HEADS * DN_DK
MIX_QKV = slice(0, 3 * _N_DN)
MIX_Z = slice(MIX_QKV.stop, MIX_QKV.stop + DN_HEADS * DN_DV)
MIX_SWQ = slice(MIX_Z.stop, MIX_Z.stop + SWA_HEADS * SWA_DH)
MIX_SWKV = slice(MIX_SWQ.stop, MIX_SWQ.stop + 2 * SWA_KV_HEADS * SWA_DH)
MIX_BA = slice(MIX_SWKV.stop, MIX_SWKV.stop + 2 * LANES)
MIX_WIDTH = MIX_BA.stop
MRG_XAQ = slice(0, XA_HEADS * XA_DH)
MRG_GATE0 = MRG_XAQ.stop


def _dot(a, b):
    return jnp.dot(a, b, preferred_element_type=F32)


def _dot_nt(a, b):
    return lax.dot_general(a, b, (((1,), (1,)), ((), ())), preferred_element_type=F32)


def _dot_tn(a, b):
    return lax.dot_general(a, b, (((0,), (0,)), ((), ())), preferred_element_type=F32)


def _layer_norm(y, g, b):
    mu = jnp.mean(y, axis=-1, keepdims=True)
    d = y - mu
    var = jnp.mean(d * d, axis=-1, keepdims=True)
    return d * lax.rsqrt(var + LN_EPS) * g + b


def _silu(x):
    return x * jax.nn.sigmoid(x)


def _ffn_ln_kernel(x_ref, wgu_ref, wd_ref, g_ref, b_ref, o_ref, acc_ref, *, alpha):
    d_ff = wd_ref.shape[0]
    xf = x_ref[...]
    xb = xf.astype(BF16)
    for c in range(d_ff // FF_CHUNK):
        lo = c * FF_CHUNK
        gate = _dot(xb, wgu_ref[:, lo:lo + FF_CHUNK])
        up = _dot(xb, wgu_ref[:, d_ff + lo:d_ff + lo + FF_CHUNK])
        act = (_silu(gate) * up).astype(BF16)
        part = _dot(act, wd_ref[lo:lo + FF_CHUNK, :])
        if c == 0:
            acc_ref[...] = part
        else:
            acc_ref[...] += part
    y = alpha * xf + 0.5 * acc_ref[...]
    o_ref[...] = _layer_norm(y, g_ref[...], b_ref[...])


def _ffn_ln(x2d, w_gu, w_down, layer, g, b, alpha):
    t, d = x2d.shape
    _, d_ff, _ = w_down.shape
    tm = min(FFN_TOKEN_TILE, t)
    return pl.pallas_call(
        functools.partial(_ffn_ln_kernel, alpha=alpha),
        out_shape=jax.ShapeDtypeStruct((t, d), F32),
        grid=(t // tm,),
        in_specs=[
            pl.BlockSpec((tm, d), lambda i: (i, 0)),
            pl.BlockSpec((None, d, 2 * d_ff), lambda i: (layer, 0, 0), pipeline_mode=pl.Buffered(1)),
            pl.BlockSpec((None, d_ff, d), lambda i: (layer, 0, 0), pipeline_mode=pl.Buffered(1)),
            pl.BlockSpec((1, d), lambda i: (0, 0)),
            pl.BlockSpec((1, d), lambda i: (0, 0)),
        ],
        out_specs=pl.BlockSpec((tm, d), lambda i: (i, 0)),
        scratch_shapes=[pltpu.VMEM((tm, d), F32)],
        compiler_params=pltpu.CompilerParams(
            dimension_semantics=("arbitrary",), vmem_limit_bytes=VMEM_LIMIT_BYTES),
        name="ffn_ln",
    )(x2d, w_gu, w_down, g, b)


def _split_bf16(x):
    hi = x.astype(BF16)
    lo = (x - hi.astype(F32)).astype(BF16)
    return hi, lo


def _split3_bf16(x):
    p1 = x.astype(BF16)
    r1 = x - p1.astype(F32)
    p2 = r1.astype(BF16)
    p3 = (r1 - p2.astype(F32)).astype(BF16)
    return p1, p2, p3


def _block_diag(y_l, n_blocks):
    c = y_l.shape[0]
    lane_block = lax.broadcasted_iota(jnp.int32, y_l.shape, 1) // c
    zero = jnp.zeros_like(y_l)
    return jnp.concatenate([jnp.where(lane_block == b, y_l, zero) for b in range(n_blocks)], axis=0)


def _blockwise_matmul(l_l, y_l, n_blocks):
    lh, ll = _split_bf16(l_l)
    yh, yl = _split_bf16(y_l)
    m = l_l.shape[0]
    top = _dot(jnp.concatenate([lh, ll], axis=0), _block_diag(yh, n_blocks))
    return top[:m] + top[m:] + _dot(lh, _block_diag(yl, n_blocks))


def _unit_lower_inverses(a_ls, n_blocks):
    c = a_ls[0].shape[0]
    row = lax.broadcasted_iota(jnp.int32, a_ls[0].shape, 0)
    col = lax.broadcasted_iota(jnp.int32, a_ls[0].shape, 1) % c
    eye = (row == col).astype(F32)
    ps = [eye - a_l for a_l in a_ls]
    ys = [_blockwise_matmul(-a_l, -a_l, n_blocks) for a_l in a_ls]
    power = 2
    while power < c:
        if 2 * power < c:
            stacked = [_blockwise_matmul(jnp.concatenate([p, y], axis=0), y, n_blocks)
                       for p, y in zip(ps, ys)]
            ps = [p + s[:c] for p, s in zip(ps, stacked)]
            ys = [s[c:] for s in stacked]
        else:
            ps = [p + _blockwise_matmul(p, y, n_blocks) for p, y in zip(ps, ys)]
        power *= 2
    return ps


def _mixers_kernel(h_ref, w_ref, convw_ref, alog_ref,
                   dtb_ref, normw_ref, sinks_ref, odn_ref, osw_ref,
                   xbuf, q_s, k_s, kb_s, qd_s, kt_s, rhs_s, z_s, gcum_s, gcol_s, elast_s, o0_s, pq_s, n_s, state, kvbuf, qlo_s, qhi_s,
                   *, tm):
    j = pl.program_id(1)
    n_qkv = 3 * DN_HEADS * DN_DK
    n_dn = DN_HEADS * DN_DK
    halo = SUBLANES
    c_ = DN_CHUNK
    n_chunks = tm // c_
    gate_lane = DN_HEADS

    @pl.when(j == 0)
    def _reset():
        xbuf[0:halo, :] = jnp.zeros((halo, n_qkv), F32)
        state[...] = jnp.zeros_like(state)
        kvbuf[0:WINDOW, :] = jnp.zeros((WINDOW, kvbuf.shape[1]), BF16)

    hb = h_ref[0].astype(BF16)

    pre = _dot(hb, w_ref[:, MIX_QKV])
    xbuf[halo:halo + tm, :] = pre
    cw = convw_ref[...]
    conv = cw[DN_CONV - 1:DN_CONV, :] * pre
    for s in range(1, DN_CONV):
        conv = conv + cw[DN_CONV - 1 - s:DN_CONV - s, :] * xbuf[halo - s:halo - s + tm, :]
    xbuf[0:halo, :] = xbuf[tm:tm + halo, :]
    qkv = _silu(conv)
    z_s[...] = _silu(_dot(hb, w_ref[:, MIX_Z]))

    ba = _dot(hb, w_ref[:, MIX_BA])
    beta = jax.nn.sigmoid(ba[:, :LANES])
    g = -jnp.exp(alog_ref[...]) * jax.nn.softplus(ba[:, LANES:] + dtb_ref[...])
    row_c = lax.broadcasted_iota(jnp.int32, (c_, c_), 0)
    col_c = lax.broadcasted_iota(jnp.int32, (c_, c_), 1)
    tril_b = (row_c >= col_c).astype(BF16)
    g_parts = _split3_bf16(g)
    gcum_parts, etail_parts, elast_parts = [], [], []
    for ci in range(n_chunks):
        gc = sum(_dot(tril_b, part[ci * c_:(ci + 1) * c_, :]) for part in g_parts)
        glast = gc[c_ - 1:c_, :]
        gcum_parts.append(gc)
        etail_parts.append(jnp.exp(glast - gc))
        elast_parts.append(jnp.broadcast_to(jnp.exp(glast), (c_, LANES)))
    gcum = jnp.concatenate(gcum_parts, axis=0)
    gcum_s[...] = gcum
    elast_s[...] = jnp.concatenate(elast_parts, axis=0)
    sel_k = lax.broadcasted_iota(jnp.int32, (LANES, DN_HEADS * c_), 0)
    sel_n = lax.broadcasted_iota(jnp.int32, (LANES, DN_HEADS * c_), 1)
    sel_b = (sel_k == gate_lane + sel_n // c_).astype(BF16)
    gcol_s[...] = sum(_dot(part, sel_b) for part in _split3_bf16(gcum))
    bk = lax.broadcasted_iota(jnp.int32, (LANES, n_dn), 0)
    bn = lax.broadcasted_iota(jnp.int32, (LANES, n_dn), 1)
    spread = (bk == gate_lane + bn // DN_DK).astype(BF16)
    stacked = jnp.concatenate([beta, jnp.exp(gcum), jnp.concatenate(etail_parts, axis=0)], axis=0)
    bc = _dot(stacked.astype(BF16), spread)
    for hd in range(DN_HEADS):
        lo = hd * DN_DK
        hs = slice(lo, lo + DN_DK)
        qh = qkv[:, lo:lo + DN_DK]
        qh = qh * (lax.rsqrt(jnp.sum(qh * qh, -1, keepdims=True) + RMS_EPS) * (DN_DK ** -0.5))
        kh = qkv[:, n_dn + lo:n_dn + lo + DN_DK]
        kh = kh * lax.rsqrt(jnp.sum(kh * kh, -1, keepdims=True) + RMS_EPS)
        vh = qkv[:, 2 * n_dn + lo:2 * n_dn + lo + DN_DV]
        beta_b = bc[0:tm, hs]
        eg_b = bc[tm:2 * tm, hs]
        et_b = bc[2 * tm:3 * tm, hs]
        kb = kh * beta_b
        q_s[:, hs] = qh.astype(BF16)
        k_s[:, hs] = kh.astype(BF16)
        kb_s[:, hs] = kb.astype(BF16)
        qd_s[:, hs] = (qh * eg_b).astype(BF16)
        kt_s[:, hs] = (kh * et_b).astype(BF16)
        rhs_s[:, 2 * lo:2 * lo + DN_DV] = (vh * beta_b).astype(BF16)
        rhs_s[:, 2 * lo + DN_DV:2 * lo + DN_DV + DN_DK] = (kb * eg_b).astype(BF16)

    stack_w = DN_HEADS * c_
    row_l = lax.broadcasted_iota(jnp.int32, (c_, stack_w), 0)
    col_l = lax.broadcasted_iota(jnp.int32, (c_, stack_w), 1) % c_
    tril_l = row_l >= col_l
    strict_l = row_l > col_l
    lane = lax.broadcasted_iota(jnp.int32, (c_, LANES), 1)
    head_block = lax.broadcasted_iota(jnp.int32, (c_, n_dn), 1) // DN_DK
    ones_l = jnp.ones((c_, LANES), BF16)
    normw = normw_ref[...]
    zero_rhs = jnp.zeros((c_, DN_DV + DN_DK), BF16)
    heads = range(DN_HEADS)
    hsl = [slice(hd * DN_DK, (hd + 1) * DN_DK) for hd in heads]
    pairs = [slice((hd // 2) * LANES, (hd // 2 + 1) * LANES) for hd in heads]

    def wy_group_body(gi, carry):
        g0 = pl.multiple_of(gi * (WY_GROUP * c_), WY_GROUP * c_)
        chunk_rows = [pl.ds(g0 + ci * c_, c_) for ci in range(WY_GROUP)]
        decays = []
        for rows in chunk_rows:
            gc = gcum_s[rows, :]
            g_rows = jnp.concatenate(
                [jnp.where(lane == gate_lane + hd, gc, 0.0) for hd in range(DN_HEADS)], axis=0)
            grow_l = sum(_dot_nt(ones_l, part) for part in _split3_bf16(g_rows))
            decays.append(jnp.exp(jnp.where(tril_l, gcol_s[rows, :] - grow_l, NEG_INF)))
        kqs = []
        for rows in chunk_rows:
            k_rows = k_s[rows, :]
            k_bd = jnp.concatenate(
                [jnp.where(head_block == hd, k_rows, jnp.zeros_like(k_rows)) for hd in range(DN_HEADS)],
                axis=0)
            kqs.append(_dot_nt(jnp.concatenate([kb_s[rows, :], q_s[rows, :]], axis=0), k_bd))
        a_ls = [jnp.where(strict_l, kq[:c_] * dec, 0.0) for kq, dec in zip(kqs, decays)]
        qk_bs = [jnp.where(tril_l, kq[c_:] * dec, 0.0).astype(BF16) for kq, dec in zip(kqs, decays)]
        t_invs = _unit_lower_inverses(a_ls, DN_HEADS)
        sols = []
        for rows, t_inv in zip(chunk_rows, t_invs):
            t_inv_b = t_inv.astype(BF16)
            for hd in heads:
                rhs_h = rhs_s[rows, 2 * hd * DN_DK:2 * hd * DN_DK + DN_DV + DN_DK]
                rhs_pad = ([rhs_h, zero_rhs] if hd % 2 == 0 else [zero_rhs, rhs_h])
                sols.append(_dot(t_inv_b[:, pairs[hd]], jnp.concatenate(rhs_pad, axis=0)).astype(BF16))
        for n, (rows, qk_b) in enumerate(zip(chunk_rows, qk_bs)):
            for hd in heads:
                sol_b = sols[n * DN_HEADS + hd]
                sol_pad = ([sol_b, zero_rhs] if hd % 2 == 0 else [zero_rhs, sol_b])
                qk_uw = _dot(qk_b[:, pairs[hd]], jnp.concatenate(sol_pad, axis=0))
                kt_uw = _dot_tn(kt_s[rows, hsl[hd]], sol_b)
                q_eff = qd_s[rows, hsl[hd]].astype(F32) - qk_uw[:, DN_DV:]
                slot = (gi * WY_GROUP + n) * DN_HEADS + hd
                o0_s[rows, hsl[hd]] = qk_uw[:, :DN_DV]
                pq_s[slot] = jnp.concatenate([kt_uw[:, DN_DV:], q_eff], axis=0).astype(BF16)
                n_s[slot] = kt_uw[:, :DN_DV]
        return carry

    def chunk_body(ci, carry):
        r0 = pl.multiple_of(ci * c_, c_)
        rows = pl.ds(r0, c_)
        e_last = elast_s[pl.ds(r0, 1), :]
        s_old = [state[hd] for hd in heads]
        prods = [_dot(pq_s[ci * DN_HEADS + hd], s_old[hd].astype(BF16)) for hd in heads]
        for hd in heads:
            state[hd] = (s_old[hd] * e_last[:, gate_lane + hd:gate_lane + hd + 1]
                         - prods[hd][:DN_DK] + n_s[ci * DN_HEADS + hd])
        for hd in heads:
            o0_s[rows, hsl[hd]] += prods[hd][DN_DK:]
        return carry

    def deltanet_outputs():
        for hd in heads:
            o = o0_s[:, hsl[hd]]
            o = o * lax.rsqrt(jnp.mean(o * o, -1, keepdims=True) + RMS_EPS) * normw
            odn_ref[0, :, hsl[hd]] = (o * z_s[:, hsl[hd]]).astype(odn_ref.dtype)

    w_ = WINDOW
    group = SWA_HEADS // SWA_KV_HEADS
    swq = _dot(hb, w_ref[:, MIX_SWQ]) * (SWA_DH ** -0.5)
    kv = _dot(hb, w_ref[:, MIX_SWKV])
    lane_kv = lax.broadcasted_iota(jnp.int32, (tm, LANES), 1)
    first_half = lane_kv < SWA_DH
    for part in range(2):
        x = kv[:, part * LANES:(part + 1) * LANES]
        xr = pltpu.roll(x, SWA_DH, axis=1)
        dup0 = jnp.where(first_half, x, xr)
        dup1 = jnp.where(first_half, xr, x)
        kvbuf[w_:w_ + tm, (2 * part) * LANES:(2 * part + 1) * LANES] = dup0.astype(BF16)
        kvbuf[w_:w_ + tm, (2 * part + 1) * LANES:(2 * part + 2) * LANES] = dup1.astype(BF16)

    lane_q = lax.broadcasted_iota(jnp.int32, (tm, SWA_HEADS * SWA_DH), 1) % LANES
    qlo_s[...] = jnp.where(lane_q < SWA_DH, swq, 0.0).astype(BF16)
    qhi_s[...] = jnp.where(lane_q < SWA_DH, 0.0, swq).astype(BF16)

    qi = lax.broadcasted_iota(jnp.int32, (group * w_, 2 * w_), 0) % w_
    kj = lax.broadcasted_iota(jnp.int32, (group * w_, 2 * w_), 1)
    band = (kj > qi) & (kj <= qi + w_)
    lo_half = lax.broadcasted_iota(jnp.int32, (w_, LANES), 1) < SWA_DH
    sinks4 = [jnp.concatenate([jnp.full((w_, 1), sinks_ref[kvh * group + gi], F32)
                               for gi in range(group)], axis=0) for kvh in range(SWA_KV_HEADS)]
    kv_heads = range(SWA_KV_HEADS)

    ones_v = jnp.ones((2 * w_, LANES), BF16)

    problems = [(blk, kvh) for blk in range(tm // w_) for kvh in kv_heads]

    def swa_scores(blk, kvh):
        pieces = []
        for pair in range(group // 2):
            cols = slice((kvh * (group // 2) + pair) * LANES, (kvh * (group // 2) + pair + 1) * LANES)
            pieces += [qlo_s[blk * w_:(blk + 1) * w_, cols], qhi_s[blk * w_:(blk + 1) * w_, cols]]
        q4 = jnp.concatenate(pieces, axis=0)
        kd = kvbuf[blk * w_:(blk + 2) * w_, kvh * LANES:(kvh + 1) * LANES]
        mask = band & jnp.logical_or(kj >= w_, j > 0) if blk == 0 else band
        return jnp.where(mask, _dot_nt(q4, kd), NEG_INF)

    def swa_outputs():
        scores = [swa_scores(blk, kvh) for blk, kvh in problems]
        maxes = [jnp.maximum(jnp.max(s, axis=-1, keepdims=True), sinks4[kvh])
                 for s, (blk, kvh) in zip(scores, problems)]
        probs = [jnp.exp(s - m).astype(BF16) for s, m in zip(scores, maxes)]
        for p, m, (blk, kvh) in zip(probs, maxes, problems):
            vd = kvbuf[blk * w_:(blk + 2) * w_, (2 + kvh) * LANES:(3 + kvh) * LANES]
            pv = _dot(p, jnp.concatenate([vd, ones_v], axis=1))
            o4 = pv[:, :LANES] / (pv[:, LANES:] + jnp.exp(sinks4[kvh] - m))
            for pair in range(group // 2):
                cols = slice((kvh * (group // 2) + pair) * LANES, (kvh * (group // 2) + pair + 1) * LANES)
                o_pair = jnp.where(lo_half, o4[(2 * pair) * w_:(2 * pair + 1) * w_],
                                   o4[(2 * pair + 1) * w_:(2 * pair + 2) * w_])
                osw_ref[0, blk * w_:(blk + 1) * w_, cols] = o_pair.astype(osw_ref.dtype)

    lax.fori_loop(0, n_chunks // WY_GROUP, wy_group_body, 0)
    lax.fori_loop(0, n_chunks, chunk_body, 0)
    deltanet_outputs()
    swa_outputs()
    kvbuf[0:w_, :] = kvbuf[tm:tm + w_, :]


def _mixers(h, w_mix, layer, convw, alog_row, dtb_row, normw_row, sinks):
    b, s, d = h.shape
    tm = min(TOKEN_TILE, s)
    n_qkv = 3 * DN_HEADS * DN_DK
    dn_w = DN_HEADS * DN_DV
    sw_w = SWA_HEADS * SWA_DH
    n_slots = (tm // DN_CHUNK) * DN_HEADS
    full = lambda shape: pl.BlockSpec(shape, lambda bi, ji: tuple(0 for _ in shape))
    tok = lambda width: pl.BlockSpec((1, tm, width), lambda bi, ji: (bi, ji, 0))
    return pl.pallas_call(
        functools.partial(_mixers_kernel, tm=tm),
        out_shape=(jax.ShapeDtypeStruct((b, s, dn_w), BF16), jax.ShapeDtypeStruct((b, s, sw_w), BF16)),
        grid=(b, s // tm),
        in_specs=[
            tok(d),
            pl.BlockSpec((None, d, w_mix.shape[-1]), lambda bi, ji: (layer, 0, 0)),
            full(convw.shape), full(alog_row.shape), full(dtb_row.shape), full(normw_row.shape),
            pl.BlockSpec(memory_space=pltpu.SMEM),
        ],
        out_specs=(tok(dn_w), tok(sw_w)),
        scratch_shapes=[
            pltpu.VMEM((tm + SUBLANES, n_qkv), F32),
            pltpu.VMEM((tm, dn_w), BF16),
            pltpu.VMEM((tm, dn_w), BF16),
            pltpu.VMEM((tm, dn_w), BF16),
            pltpu.VMEM((tm, dn_w), BF16),
            pltpu.VMEM((tm, dn_w), BF16),
            pltpu.VMEM((tm, 2 * dn_w), BF16),
            pltpu.VMEM((tm, dn_w), F32),
            pltpu.VMEM((tm, LANES), F32),
            pltpu.VMEM((tm, DN_HEADS * DN_CHUNK), F32),
            pltpu.VMEM((tm, LANES), F32),
            pltpu.VMEM((tm, dn_w), F32),
            pltpu.VMEM((n_slots, DN_DK + DN_CHUNK, DN_DK), BF16),
            pltpu.VMEM((n_slots, DN_DK, DN_DV), F32),
            pltpu.VMEM((DN_HEADS, DN_DK, DN_DV), F32),
            pltpu.VMEM((tm + WINDOW, 4 * LANES), BF16),
            pltpu.VMEM((tm, sw_w), BF16),
            pltpu.VMEM((tm, sw_w), BF16),
        ],
        compiler_params=pltpu.CompilerParams(
            dimension_semantics=("arbitrary", "arbitrary"), vmem_limit_bytes=VMEM_LIMIT_BYTES),
        name="mixers",
    )(h, w_mix, convw, alog_row, dtb_row, normw_row, sinks)


def _merge_kernel(h_ref, odn_ref, osw_ref, mem_ref, memg_ref, memb_ref, wmkv_ref, w_ref,
                  wb_ref, wo_ref, g_ref, b_ref, o_ref, kmem, vmem, merged, *, alpha):
    j = pl.program_id(1)
    xa_w = XA_HEADS * XA_DH

    @pl.when(j == 0)
    def _memory_kv():
        mem_n = _layer_norm(mem_ref[0], memg_ref[...], memb_ref[...])
        kvm = _dot(mem_n.astype(BF16), wmkv_ref[...])
        kmem[...] = kvm[:, :xa_w].astype(BF16)
        vmem[...] = kvm[:, xa_w:].astype(BF16)

    hf = h_ref[0]
    hb = hf.astype(BF16)
    xq = _dot(hb, w_ref[:, MRG_XAQ]) * (XA_DH ** -0.5)
    hsl = [slice(hd * XA_DH, (hd + 1) * XA_DH) for hd in range(XA_HEADS)]
    xqb = xq.astype(BF16)
    scores = [_dot_nt(xqb[:, sl], kmem[:, sl]) for sl in hsl]
    probs = [jnp.exp(s - jnp.max(s, axis=-1, keepdims=True)) for s in scores]
    denoms = [jnp.sum(p, axis=-1, keepdims=True) for p in probs]
    heads = [_dot(p.astype(BF16), vmem[:, sl]) / dn for p, dn, sl in zip(probs, denoms, hsl)]
    oxa = jnp.concatenate(heads, axis=1).astype(BF16)
    branches = (odn_ref[0], osw_ref[0], oxa)
    d = hf.shape[-1]
    for dc in range(d // MERGE_CHUNK):
        lo = dc * MERGE_CHUNK
        acc = None
        for n in range(N_BRANCH):
            g0 = MRG_GATE0 + n * d + lo
            gate = jax.nn.sigmoid(_dot(hb, w_ref[:, g0:g0 + MERGE_CHUNK]))
            term = gate * _dot(branches[n], wb_ref[n, :, lo:lo + MERGE_CHUNK])
            acc = term if acc is None else acc + term
        merged[:, lo:lo + MERGE_CHUNK] = acc.astype(BF16)
    y = alpha * hf + _dot(merged[...], wo_ref[...])
    o_ref[0] = _layer_norm(y, g_ref[...], b_ref[...])


def _merge(h, odn, osw, mem, memg, memb, wmkv, w_mrg, wbranch, wout, layer, g, b_, alpha):
    b, s, d = h.shape
    tm = min(TOKEN_TILE, s)
    n_mem = mem.shape[1]
    xa_w = XA_HEADS * XA_DH
    full = lambda shape: pl.BlockSpec(shape, lambda bi, ji: tuple(0 for _ in shape))
    tok = lambda width: pl.BlockSpec((1, tm, width), lambda bi, ji: (bi, ji, 0))
    layer_block = lambda w: pl.BlockSpec((None,) + w.shape[1:],
                                         lambda bi, ji: (layer,) + (0,) * (w.ndim - 1))
    return pl.pallas_call(
        functools.partial(_merge_kernel, alpha=alpha),
        out_shape=jax.ShapeDtypeStruct((b, s, d), F32),
        grid=(b, s // tm),
        in_specs=[
            tok(d), tok(odn.shape[-1]), tok(osw.shape[-1]),
            pl.BlockSpec((1, n_mem, d), lambda bi, ji: (bi, 0, 0)),
            full(memg.shape), full(memb.shape), layer_block(wmkv), layer_block(w_mrg),
            layer_block(wbranch), layer_block(wout), full(g.shape), full(b_.shape),
        ],
        out_specs=tok(d),
        scratch_shapes=[
            pltpu.VMEM((n_mem, xa_w), BF16),
            pltpu.VMEM((n_mem, xa_w), BF16),
            pltpu.VMEM((tm, d), BF16),
        ],
        compiler_params=pltpu.CompilerParams(
            dimension_semantics=("arbitrary", "arbitrary"), vmem_limit_bytes=VMEM_LIMIT_BYTES),
        name="merge",
    )(h, odn, osw, mem, memg, memb, wmkv, w_mrg, wbranch, wout, g, b_)


def _lane_row(vals, offset):
    return jnp.zeros((1, LANES), F32).at[0, offset:offset + vals.shape[0]].set(vals.astype(F32))


def _pack_input_projection(w_in):
    depth, d, _ = w_in.shape
    o_b = 3 * _N_DN
    o_a = o_b + DN_HEADS
    o_z = o_a + DN_HEADS
    o_swq = o_z + DN_HEADS * DN_DV
    o_swk = o_swq + SWA_HEADS * SWA_DH
    o_xaq = o_swk + 2 * SWA_KV_HEADS * SWA_DH
    zeros = lambda n: jnp.zeros((depth, d, n), w_in.dtype)
    ba = [zeros(DN_HEADS), w_in[..., o_b:o_a], zeros(LANES - 2 * DN_HEADS),
          zeros(DN_HEADS), w_in[..., o_a:o_z], zeros(LANES - 2 * DN_HEADS)]
    w_mix = jnp.concatenate([w_in[..., :o_b], w_in[..., o_z:o_xaq]] + ba, axis=-1).astype(BF16)
    assert w_mix.shape[-1] == MIX_WIDTH
    w_mrg = w_in[..., o_xaq:].astype(BF16)
    return w_mix, w_mrg


def _mixer_tables(conv_w, a_log, dt_bias, norm_w, sinks):
    return (conv_w.astype(F32), _lane_row(a_log, DN_HEADS), _lane_row(dt_bias, DN_HEADS),
            norm_w.reshape(1, -1).astype(F32), sinks.astype(F32))


def kernel(x, mem, mem_ln_g, mem_ln_b, ln_g, ln_b, ffn1_w_gu, ffn1_w_down, w_in, dn_conv_w, dn_a_log,
           dn_dt_bias, dn_norm_w, swa_sinks, w_mem_kv, w_branch, w_out, ffn2_w_gu, ffn2_w_down):
    b, s, d = x.shape
    depth = ln_g.shape[0]
    alpha = float((2 * depth) ** 0.25)
    row = lambda v: v.reshape(1, -1).astype(F32)

    w_mix, w_mrg = _pack_input_projection(w_in)
    ffn1_gu, ffn1_down = ffn1_w_gu.astype(BF16), ffn1_w_down.astype(BF16)
    ffn2_gu, ffn2_down = ffn2_w_gu.astype(BF16), ffn2_w_down.astype(BF16)
    w_mem_kv_b, w_branch_b, w_out_b = w_mem_kv.astype(BF16), w_branch.astype(BF16), w_out.astype(BF16)

    for l in range(depth):
        h = _ffn_ln(x.reshape(b * s, d), ffn1_gu, ffn1_down, l, row(ln_g[l, 0]), row(ln_b[l, 0]), alpha)
        h = h.reshape(b, s, d)
        odn, osw = _mixers(h, w_mix, l, *_mixer_tables(dn_conv_w[l], dn_a_log[l], dn_dt_bias[l],
                                                       dn_norm_w[l], swa_sinks[l]))
        h = _merge(h, odn, osw, mem, row(mem_ln_g), row(mem_ln_b), w_mem_kv_b, w_mrg, w_branch_b,
                   w_out_b, l, row(ln_g[l, 1]), row(ln_b[l, 1]), alpha)
        x = _ffn_ln(h.reshape(b * s, d), ffn2_gu, ffn2_down, l, row(ln_g[l, 2]), row(ln_b[l, 2]), alpha)
        x = x.reshape(b, s, d)
    return x
```

```python
import functools

import jax
import jax.numpy as jnp
from jax import lax
from jax.experimental import pallas as pl
from jax.experimental.pallas import tpu as pltpu

F32 = jnp.float32
BF16 = jnp.bfloat16

DN_HEADS, DN_DK, DN_DV, DN_CONV, DN_CHUNK = 4, 128, 128, 4, 64
SWA_HEADS, SWA_KV_HEADS, SWA_DH, WINDOW = 8, 2, 64, 128
XA_HEADS, XA_DH = 4, 128
N_BRANCH, BRANCH_W = 3, 512
LN_EPS, RMS_EPS, NEG_INF = 1e-5, 1e-6, -1e30

LANES = 128
SUBLANES = 8
VMEM_LIMIT_BYTES = 56 * 1024 * 1024

TOKEN_TILE = 512
FFN_TOKEN_TILE = 1024
FF_CHUNK = 256
MERGE_CHUNK = 256
WY_GROUP = 8

_N_DN = DN_HEADS * DN_DK
MIX_QKV = slice(0, 3 * _N_DN)
MIX_Z = slice(MIX_QKV.stop, MIX_QKV.stop + DN_HEADS * DN_DV)
MIX_SWQ = slice(MIX_Z.stop, MIX_Z.stop + SWA_HEADS * SWA_DH)
MIX_SWKV = slice(MIX_SWQ.stop, MIX_SWQ.stop + 2 * SWA_KV_HEADS * SWA_DH)
MIX_BA = slice(MIX_SWKV.stop, MIX_SWKV.stop + 2 * LANES)
MIX_WIDTH = MIX_BA.stop
MRG_XAQ = slice(0, XA_HEADS * XA_DH)
MRG_GATE0 = MRG_XAQ.stop


def _dot(a, b):
    return jnp.dot(a, b, preferred_element_type=F32)


def _dot_nt(a, b):
    return lax.dot_general(a, b, (((1,), (1,)), ((), ())), preferred_element_type=F32)


def _dot_tn(a, b):
    return lax.dot_general(a, b, (((0,), (0,)), ((), ())), preferred_element_type=F32)


def _layer_norm(y, g, b):
    mu = jnp.mean(y, axis=-1, keepdims=True)
    d = y - mu
    var = jnp.mean(d * d, axis=-1, keepdims=True)
    return d * lax.rsqrt(var + LN_EPS) * g + b


def _sigmoid(x):
    return 0.5 * (jnp.tanh(0.5 * x) + 1.0)


def _silu(x):
    half = 0.5 * x
    return half * (jnp.tanh(half) + 1.0)


def _ffn_ln_kernel(x_ref, wgu_ref, wd_ref, g_ref, b_ref, o_ref, acc_ref, *, alpha):
    d_ff = wd_ref.shape[0]
    xf = x_ref[...]
    xb = xf.astype(BF16)
    for c in range(d_ff // FF_CHUNK):
        lo = c * FF_CHUNK
        gate = _dot(xb, wgu_ref[:, lo:lo + FF_CHUNK])
        up = _dot(xb, wgu_ref[:, d_ff + lo:d_ff + lo + FF_CHUNK])
        act = (_silu(gate) * up).astype(BF16)
        part = _dot(act, wd_ref[lo:lo + FF_CHUNK, :])
        if c == 0:
            acc_ref[...] = part
        else:
            acc_ref[...] += part
    y = alpha * xf + 0.5 * acc_ref[...]
    o_ref[...] = _layer_norm(y, g_ref[...], b_ref[...])


def _ffn_ln(x2d, w_gu, w_down, layer, g, b, alpha):
    t, d = x2d.shape
    _, d_ff, _ = w_down.shape
    tm = min(FFN_TOKEN_TILE, t)
    return pl.pallas_call(
        functools.partial(_ffn_ln_kernel, alpha=alpha),
        out_shape=jax.ShapeDtypeStruct((t, d), F32),
        grid=(t // tm,),
        in_specs=[
            pl.BlockSpec((tm, d), lambda i: (i, 0)),
            pl.BlockSpec((None, d, 2 * d_ff), lambda i: (layer, 0, 0), pipeline_mode=pl.Buffered(1)),
            pl.BlockSpec((None, d_ff, d), lambda i: (layer, 0, 0), pipeline_mode=pl.Buffered(1)),
            pl.BlockSpec((1, d), lambda i: (0, 0)),
            pl.BlockSpec((1, d), lambda i: (0, 0)),
        ],
        out_specs=pl.BlockSpec((tm, d), lambda i: (i, 0)),
        scratch_shapes=[pltpu.VMEM((tm, d), F32)],
        compiler_params=pltpu.CompilerParams(
            dimension_semantics=("arbitrary",), vmem_limit_bytes=VMEM_LIMIT_BYTES),
        name="ffn_ln",
    )(x2d, w_gu, w_down, g, b)


def _split_bf16(x):
    hi = x.astype(BF16)
    lo = (x - hi.astype(F32)).astype(BF16)
    return hi, lo


def _split3_bf16(x):
    p1 = x.astype(BF16)
    r1 = x - p1.astype(F32)
    p2 = r1.astype(BF16)
    p3 = (r1 - p2.astype(F32)).astype(BF16)
    return p1, p2, p3


def _block_diag(y_l, n_blocks):
    c = y_l.shape[0]
    lane_block = lax.broadcasted_iota(jnp.int32, y_l.shape, 1) // c
    zero = jnp.zeros_like(y_l)
    return jnp.concatenate([jnp.where(lane_block == b, y_l, zero) for b in range(n_blocks)], axis=0)


def _blockwise_matmul(l_l, y_l, n_blocks):
    lh, ll = _split_bf16(l_l)
    yh, yl = _split_bf16(y_l)
    m = l_l.shape[0]
    top = _dot(jnp.concatenate([lh, ll], axis=0), _block_diag(yh, n_blocks))
    return top[:m] + top[m:] + _dot(lh, _block_diag(yl, n_blocks))


def _unit_lower_inverses(a_ls, n_blocks):
    c = a_ls[0].shape[0]
    row = lax.broadcasted_iota(jnp.int32, a_ls[0].shape, 0)
    col = lax.broadcasted_iota(jnp.int32, a_ls[0].shape, 1) % c
    eye = (row == col).astype(F32)
    ps = [eye - a_l for a_l in a_ls]
    ys = [_blockwise_matmul(-a_l, -a_l, n_blocks) for a_l in a_ls]
    power = 2
    while power < c:
        if 2 * power < c:
            stacked = [_blockwise_matmul(jnp.concatenate([p, y], axis=0), y, n_blocks)
                       for p, y in zip(ps, ys)]
            ps = [p + s[:c] for p, s in zip(ps, stacked)]
            ys = [s[c:] for s in stacked]
        else:
            ps = [p + _blockwise_matmul(p, y, n_blocks) for p, y in zip(ps, ys)]
        power *= 2
    return ps


def _mixers_kernel(h_ref, w_ref, convw_ref, alog_ref,
                   dtb_ref, normw_ref, sinks_ref, odn_ref, osw_ref,
                   xbuf, q_s, k_s, kb_s, qd_s, kt_s, rhs_s, z_s, gcum_s, gcol_s, elast_s, o0_s, pq_s, n_s, state, kvbuf, qlo_s, qhi_s,
                   *, tm):
    j = pl.program_id(1)
    n_qkv = 3 * DN_HEADS * DN_DK
    n_dn = DN_HEADS * DN_DK
    halo = SUBLANES
    c_ = DN_CHUNK
    n_chunks = tm // c_
    gate_lane = DN_HEADS

    @pl.when(j == 0)
    def _reset():
        xbuf[0:halo, :] = jnp.zeros((halo, n_qkv), F32)
        state[...] = jnp.zeros_like(state)
        kvbuf[0:WINDOW, :] = jnp.zeros((WINDOW, kvbuf.shape[1]), BF16)

    hb = h_ref[0].astype(BF16)

    pre = _dot(hb, w_ref[:, MIX_QKV])
    xbuf[halo:halo + tm, :] = pre
    cw = convw_ref[...]
    conv = cw[DN_CONV - 1:DN_CONV, :] * pre
    for s in range(1, DN_CONV):
        conv = conv + cw[DN_CONV - 1 - s:DN_CONV - s, :] * xbuf[halo - s:halo - s + tm, :]
    xbuf[0:halo, :] = xbuf[tm:tm + halo, :]
    qkv = _silu(conv)
    z_s[...] = _silu(_dot(hb, w_ref[:, MIX_Z]))

    ba = _dot(hb, w_ref[:, MIX_BA])
    beta = _sigmoid(ba[:, :LANES])
    g = -jnp.exp(alog_ref[...]) * jax.nn.softplus(ba[:, LANES:] + dtb_ref[...])
    row_c = lax.broadcasted_iota(jnp.int32, (c_, c_), 0)
    col_c = lax.broadcasted_iota(jnp.int32, (c_, c_), 1)
    tril_b = (row_c >= col_c).astype(BF16)
    g_parts = _split3_bf16(g)
    gcum_parts, etail_parts, elast_parts = [], [], []
    for ci in range(n_chunks):
        gc = sum(_dot(tril_b, part[ci * c_:(ci + 1) * c_, :]) for part in g_parts)
        glast = gc[c_ - 1:c_, :]
        gcum_parts.append(gc)
        etail_parts.append(jnp.exp(glast - gc))
        elast_parts.append(jnp.broadcast_to(jnp.exp(glast), (c_, LANES)))
    gcum = jnp.concatenate(gcum_parts, axis=0)
    gcum_s[...] = gcum
    elast_s[...] = jnp.concatenate(elast_parts, axis=0)
    sel_k = lax.broadcasted_iota(jnp.int32, (LANES, DN_HEADS * c_), 0)
    sel_n = lax.broadcasted_iota(jnp.int32, (LANES, DN_HEADS * c_), 1)
    sel_b = (sel_k == gate_lane + sel_n // c_).astype(BF16)
    gcol_s[...] = sum(_dot(part, sel_b) for part in _split3_bf16(gcum))
    e_g = jnp.exp(gcum)
    e_tail = jnp.concatenate(etail_parts, axis=0)
    for hd in range(DN_HEADS):
        lo = hd * DN_DK
        hs = slice(lo, lo + DN_DK)
        qh = qkv[:, lo:lo + DN_DK]
        qh = qh * (lax.rsqrt(jnp.sum(qh * qh, -1, keepdims=True) + RMS_EPS) * (DN_DK ** -0.5))
        kh = qkv[:, n_dn + lo:n_dn + lo + DN_DK]
        kh = kh * lax.rsqrt(jnp.sum(kh * kh, -1, keepdims=True) + RMS_EPS)
        vh = qkv[:, 2 * n_dn + lo:2 * n_dn + lo + DN_DV]
        gl = gate_lane + hd
        beta_b = jnp.broadcast_to(beta[:, gl:gl + 1], (tm, DN_DK))
        eg_b = jnp.broadcast_to(e_g[:, gl:gl + 1], (tm, DN_DK))
        et_b = jnp.broadcast_to(e_tail[:, gl:gl + 1], (tm, DN_DK))
        kb = kh * beta_b
        q_s[:, hs] = qh.astype(BF16)
        k_s[:, hs] = kh.astype(BF16)
        kb_s[:, hs] = kb.astype(BF16)
        qd_s[:, hs] = (qh * eg_b).astype(BF16)
        kt_s[:, hs] = (kh * et_b).astype(BF16)
        rhs_s[:, 2 * lo:2 * lo + DN_DV] = (vh * beta_b).astype(BF16)
        rhs_s[:, 2 * lo + DN_DV:2 * lo + DN_DV + DN_DK] = (kb * eg_b).astype(BF16)

    stack_w = DN_HEADS * c_
    row_l = lax.broadcasted_iota(jnp.int32, (c_, stack_w), 0)
    col_l = lax.broadcasted_iota(jnp.int32, (c_, stack_w), 1) % c_
    tril_l = row_l >= col_l
    strict_l = row_l > col_l
    lane = lax.broadcasted_iota(jnp.int32, (c_, LANES), 1)
    head_block = lax.broadcasted_iota(jnp.int32, (c_, n_dn), 1) // DN_DK
    ones_l = jnp.ones((c_, LANES), BF16)
    normw = normw_ref[...]
    zero_rhs = jnp.zeros((c_, DN_DV + DN_DK), BF16)
    heads = range(DN_HEADS)
    hsl = [slice(hd * DN_DK, (hd + 1) * DN_DK) for hd in heads]
    pairs = [slice((hd // 2) * LANES, (hd // 2 + 1) * LANES) for hd in heads]

    def wy_group_body(gi, carry):
        g0 = pl.multiple_of(gi * (WY_GROUP * c_), WY_GROUP * c_)
        chunk_rows = [pl.ds(g0 + ci * c_, c_) for ci in range(WY_GROUP)]
        decays = []
        for rows in chunk_rows:
            gc = gcum_s[rows, :]
            g_rows = jnp.concatenate(
                [jnp.where(lane == gate_lane + hd, gc, 0.0) for hd in range(DN_HEADS)], axis=0)
            grow_l = sum(_dot_nt(ones_l, part) for part in _split3_bf16(g_rows))
            decays.append(jnp.exp(jnp.where(tril_l, gcol_s[rows, :] - grow_l, NEG_INF)))
        kqs = []
        for rows in chunk_rows:
            k_rows = k_s[rows, :]
            k_bd = jnp.concatenate(
                [jnp.where(head_block == hd, k_rows, jnp.zeros_like(k_rows)) for hd in range(DN_HEADS)],
                axis=0)
            kqs.append(_dot_nt(jnp.concatenate([kb_s[rows, :], q_s[rows, :]], axis=0), k_bd))
        a_ls = [jnp.where(strict_l, kq[:c_] * dec, 0.0) for kq, dec in zip(kqs, decays)]
        qk_bs = [jnp.where(tril_l, kq[c_:] * dec, 0.0).astype(BF16) for kq, dec in zip(kqs, decays)]
        t_invs = _unit_lower_inverses(a_ls, DN_HEADS)
        sols = []
        for rows, t_inv in zip(chunk_rows, t_invs):
            t_inv_b = t_inv.astype(BF16)
            for hd in heads:
                rhs_h = rhs_s[rows, 2 * hd * DN_DK:2 * hd * DN_DK + DN_DV + DN_DK]
                rhs_pad = ([rhs_h, zero_rhs] if hd % 2 == 0 else [zero_rhs, rhs_h])
                sols.append(_dot(t_inv_b[:, pairs[hd]], jnp.concatenate(rhs_pad, axis=0)).astype(BF16))
        for n, (rows, qk_b) in enumerate(zip(chunk_rows, qk_bs)):
            for hd in heads:
                sol_b = sols[n * DN_HEADS + hd]
                sol_pad = ([sol_b, zero_rhs] if hd % 2 == 0 else [zero_rhs, sol_b])
                qk_uw = _dot(qk_b[:, pairs[hd]], jnp.concatenate(sol_pad, axis=0))
                kt_uw = _dot_tn(kt_s[rows, hsl[hd]], sol_b)
                q_eff = qd_s[rows, hsl[hd]].astype(F32) - qk_uw[:, DN_DV:]
                slot = (gi * WY_GROUP + n) * DN_HEADS + hd
                o0_s[rows, hsl[hd]] = qk_uw[:, :DN_DV]
                pq_s[slot] = jnp.concatenate([kt_uw[:, DN_DV:], q_eff], axis=0).astype(BF16)
                n_s[slot] = kt_uw[:, :DN_DV]
        return carry

    def chunk_body(ci, carry):
        r0 = pl.multiple_of(ci * c_, c_)
        rows = pl.ds(r0, c_)
        e_last = elast_s[pl.ds(r0, 1), :]
        s_old = [state[hd] for hd in heads]
        prods = [_dot(pq_s[ci * DN_HEADS + hd], s_old[hd].astype(BF16)) for hd in heads]
        for hd in heads:
            state[hd] = (s_old[hd] * e_last[:, gate_lane + hd:gate_lane + hd + 1]
                         - prods[hd][:DN_DK] + n_s[ci * DN_HEADS + hd])
        for hd in heads:
            o0_s[rows, hsl[hd]] += prods[hd][DN_DK:]
        return carry

    def deltanet_outputs():
        for hd in heads:
            o = o0_s[:, hsl[hd]]
            o = o * lax.rsqrt(jnp.mean(o * o, -1, keepdims=True) + RMS_EPS) * normw
            odn_ref[0, :, hsl[hd]] = (o * z_s[:, hsl[hd]]).astype(odn_ref.dtype)

    w_ = WINDOW
    group = SWA_HEADS // SWA_KV_HEADS
    swq = _dot(hb, w_ref[:, MIX_SWQ]) * (SWA_DH ** -0.5)
    kv = _dot(hb, w_ref[:, MIX_SWKV])
    lane_kv = lax.broadcasted_iota(jnp.int32, (tm, LANES), 1)
    first_half = lane_kv < SWA_DH
    for part in range(2):
        x = kv[:, part * LANES:(part + 1) * LANES]
        xr = pltpu.roll(x, SWA_DH, axis=1)
        dup0 = jnp.where(first_half, x, xr)
        dup1 = jnp.where(first_half, xr, x)
        kvbuf[w_:w_ + tm, (2 * part) * LANES:(2 * part + 1) * LANES] = dup0.astype(BF16)
        kvbuf[w_:w_ + tm, (2 * part + 1) * LANES:(2 * part + 2) * LANES] = dup1.astype(BF16)

    lane_q = lax.broadcasted_iota(jnp.int32, (tm, SWA_HEADS * SWA_DH), 1) % LANES
    qlo_s[...] = jnp.where(lane_q < SWA_DH, swq, 0.0).astype(BF16)
    qhi_s[...] = jnp.where(lane_q < SWA_DH, 0.0, swq).astype(BF16)

    qi = lax.broadcasted_iota(jnp.int32, (group * w_, 2 * w_), 0) % w_
    kj = lax.broadcasted_iota(jnp.int32, (group * w_, 2 * w_), 1)
    band = (kj > qi) & (kj <= qi + w_)
    lo_half = lax.broadcasted_iota(jnp.int32, (w_, LANES), 1) < SWA_DH
    sinks4 = [jnp.concatenate([jnp.full((w_, 1), sinks_ref[kvh * group + gi], F32)
                               for gi in range(group)], axis=0) for kvh in range(SWA_KV_HEADS)]
    kv_heads = range(SWA_KV_HEADS)

    ones_v = jnp.ones((2 * w_, LANES), BF16)

    problems = [(blk, kvh) for blk in range(tm // w_) for kvh in kv_heads]

    def swa_scores(blk, kvh):
        pieces = []
        for pair in range(group // 2):
            cols = slice((kvh * (group // 2) + pair) * LANES, (kvh * (group // 2) + pair + 1) * LANES)
            pieces += [qlo_s[blk * w_:(blk + 1) * w_, cols], qhi_s[blk * w_:(blk + 1) * w_, cols]]
        q4 = jnp.concatenate(pieces, axis=0)
        kd = kvbuf[blk * w_:(blk + 2) * w_, kvh * LANES:(kvh + 1) * LANES]
        mask = band & jnp.logical_or(kj >= w_, j > 0) if blk == 0 else band
        return jnp.where(mask, _dot_nt(q4, kd), NEG_INF)

    def swa_outputs():
        scores = [swa_scores(blk, kvh) for blk, kvh in problems]
        maxes = [jnp.maximum(jnp.max(s, axis=-1, keepdims=True), sinks4[kvh])
                 for s, (blk, kvh) in zip(scores, problems)]
        probs = [jnp.exp(s - m).astype(BF16) for s, m in zip(scores, maxes)]
        for p, m, (blk, kvh) in zip(probs, maxes, problems):
            vd = kvbuf[blk * w_:(blk + 2) * w_, (2 + kvh) * LANES:(3 + kvh) * LANES]
            pv = _dot(p, jnp.concatenate([vd, ones_v], axis=1))
            o4 = pv[:, :LANES] / (pv[:, LANES:] + jnp.exp(sinks4[kvh] - m))
            for pair in range(group // 2):
                cols = slice((kvh * (group // 2) + pair) * LANES, (kvh * (group // 2) + pair + 1) * LANES)
                o_pair = jnp.where(lo_half, o4[(2 * pair) * w_:(2 * pair + 1) * w_],
                                   o4[(2 * pair + 1) * w_:(2 * pair + 2) * w_])
                osw_ref[0, blk * w_:(blk + 1) * w_, cols] = o_pair.astype(osw_ref.dtype)

    lax.fori_loop(0, n_chunks // WY_GROUP, wy_group_body, 0)
    lax.fori_loop(0, n_chunks, chunk_body, 0)
    deltanet_outputs()
    swa_outputs()
    kvbuf[0:w_, :] = kvbuf[tm:tm + w_, :]


def _mixers(h, w_mix, layer, convw, alog_row, dtb_row, normw_row, sinks):
    b, s, d = h.shape
    tm = min(TOKEN_TILE, s)
    n_qkv = 3 * DN_HEADS * DN_DK
    dn_w = DN_HEADS * DN_DV
    sw_w = SWA_HEADS * SWA_DH
    n_slots = (tm // DN_CHUNK) * DN_HEADS
    full = lambda shape: pl.BlockSpec(shape, lambda bi, ji: tuple(0 for _ in shape))
    tok = lambda width: pl.BlockSpec((1, tm, width), lambda bi, ji: (bi, ji, 0))
    return pl.pallas_call(
        functools.partial(_mixers_kernel, tm=tm),
        out_shape=(jax.ShapeDtypeStruct((b, s, dn_w), BF16), jax.ShapeDtypeStruct((b, s, sw_w), BF16)),
        grid=(b, s // tm),
        in_specs=[
            tok(d),
            pl.BlockSpec((None, d, w_mix.shape[-1]), lambda bi, ji: (layer, 0, 0)),
            full(convw.shape), full(alog_row.shape), full(dtb_row.shape), full(normw_row.shape),
            pl.BlockSpec(memory_space=pltpu.SMEM),
        ],
        out_specs=(tok(dn_w), tok(sw_w)),
        scratch_shapes=[
            pltpu.VMEM((tm + SUBLANES, n_qkv), F32),
            pltpu.VMEM((tm, dn_w), BF16),
            pltpu.VMEM((tm, dn_w), BF16),
            pltpu.VMEM((tm, dn_w), BF16),
            pltpu.VMEM((tm, dn_w), BF16),
            pltpu.VMEM((tm, dn_w), BF16),
            pltpu.VMEM((tm, 2 * dn_w), BF16),
            pltpu.VMEM((tm, dn_w), F32),
            pltpu.VMEM((tm, LANES), F32),
            pltpu.VMEM((tm, DN_HEADS * DN_CHUNK), F32),
            pltpu.VMEM((tm, LANES), F32),
            pltpu.VMEM((tm, dn_w), F32),
            pltpu.VMEM((n_slots, DN_DK + DN_CHUNK, DN_DK), BF16),
            pltpu.VMEM((n_slots, DN_DK, DN_DV), F32),
            pltpu.VMEM((DN_HEADS, DN_DK, DN_DV), F32),
            pltpu.VMEM((tm + WINDOW, 4 * LANES), BF16),
            pltpu.VMEM((tm, sw_w), BF16),
            pltpu.VMEM((tm, sw_w), BF16),
        ],
        compiler_params=pltpu.CompilerParams(
            dimension_semantics=("arbitrary", "arbitrary"), vmem_limit_bytes=VMEM_LIMIT_BYTES),
        name="mixers",
    )(h, w_mix, convw, alog_row, dtb_row, normw_row, sinks)


def _merge_kernel(h_ref, odn_ref, osw_ref, mem_ref, memg_ref, memb_ref, wmkv_ref, w_ref,
                  wb_ref, wo_ref, g_ref, b_ref, o_ref, kmem, vmem, merged, *, alpha):
    j = pl.program_id(1)
    xa_w = XA_HEADS * XA_DH

    @pl.when(j == 0)
    def _memory_kv():
        mem_n = _layer_norm(mem_ref[0], memg_ref[...], memb_ref[...])
        kvm = _dot(mem_n.astype(BF16), wmkv_ref[...])
        kmem[...] = kvm[:, :xa_w].astype(BF16)
        vmem[...] = kvm[:, xa_w:].astype(BF16)

    hf = h_ref[0]
    hb = hf.astype(BF16)
    xq = _dot(hb, w_ref[:, MRG_XAQ]) * (XA_DH ** -0.5)
    hsl = [slice(hd * XA_DH, (hd + 1) * XA_DH) for hd in range(XA_HEADS)]
    xqb = xq.astype(BF16)
    scores = [_dot_nt(xqb[:, sl], kmem[:, sl]) for sl in hsl]
    probs = [jnp.exp(s - jnp.max(s, axis=-1, keepdims=True)) for s in scores]
    denoms = [jnp.sum(p, axis=-1, keepdims=True) for p in probs]
    heads = [_dot(p.astype(BF16), vmem[:, sl]) / dn for p, dn, sl in zip(probs, denoms, hsl)]
    oxa = jnp.concatenate(heads, axis=1).astype(BF16)
    branches = (odn_ref[0], osw_ref[0], oxa)
    d = hf.shape[-1]
    for dc in range(d // MERGE_CHUNK):
        lo = dc * MERGE_CHUNK
        acc = None
        for n in range(N_BRANCH):
            g0 = MRG_GATE0 + n * d + lo
            gate = _sigmoid(_dot(hb, w_ref[:, g0:g0 + MERGE_CHUNK]))
            term = gate * _dot(branches[n], wb_ref[n, :, lo:lo + MERGE_CHUNK])
            acc = term if acc is None else acc + term
        merged[:, lo:lo + MERGE_CHUNK] = acc.astype(BF16)
    y = alpha * hf + _dot(merged[...], wo_ref[...])
    o_ref[0] = _layer_norm(y, g_ref[...], b_ref[...])


def _merge(h, odn, osw, mem, memg, memb, wmkv, w_mrg, wbranch, wout, layer, g, b_, alpha):
    b, s, d = h.shape
    tm = min(TOKEN_TILE, s)
    n_mem = mem.shape[1]
    xa_w = XA_HEADS * XA_DH
    full = lambda shape: pl.BlockSpec(shape, lambda bi, ji: tuple(0 for _ in shape))
    tok = lambda width: pl.BlockSpec((1, tm, width), lambda bi, ji: (bi, ji, 0))
    layer_block = lambda w: pl.BlockSpec((None,) + w.shape[1:],
                                         lambda bi, ji: (layer,) + (0,) * (w.ndim - 1))
    return pl.pallas_call(
        functools.partial(_merge_kernel, alpha=alpha),
        out_shape=jax.ShapeDtypeStruct((b, s, d), F32),
        grid=(b, s // tm),
        in_specs=[
            tok(d), tok(odn.shape[-1]), tok(osw.shape[-1]),
            pl.BlockSpec((1, n_mem, d), lambda bi, ji: (bi, 0, 0)),
            full(memg.shape), full(memb.shape), layer_block(wmkv), layer_block(w_mrg),
            layer_block(wbranch), layer_block(wout), full(g.shape), full(b_.shape),
        ],
        out_specs=tok(d),
        scratch_shapes=[
            pltpu.VMEM((n_mem, xa_w), BF16),
            pltpu.VMEM((n_mem, xa_w), BF16),
            pltpu.VMEM((tm, d), BF16),
        ],
        compiler_params=pltpu.CompilerParams(
            dimension_semantics=("arbitrary", "arbitrary"), vmem_limit_bytes=VMEM_LIMIT_BYTES),
        name="merge",
    )(h, odn, osw, mem, memg, memb, wmkv, w_mrg, wbranch, wout, g, b_)


def _lane_row(vals, offset):
    return jnp.zeros((1, LANES), F32).at[0, offset:offset + vals.shape[0]].set(vals.astype(F32))


def _pack_input_projection(w_in):
    depth, d, _ = w_in.shape
    o_b = 3 * _N_DN
    o_a = o_b + DN_HEADS
    o_z = o_a + DN_HEADS
    o_swq = o_z + DN_HEADS * DN_DV
    o_swk = o_swq + SWA_HEADS * SWA_DH
    o_xaq = o_swk + 2 * SWA_KV_HEADS * SWA_DH
    zeros = lambda n: jnp.zeros((depth, d, n), w_in.dtype)
    ba = [zeros(DN_HEADS), w_in[..., o_b:o_a], zeros(LANES - 2 * DN_HEADS),
          zeros(DN_HEADS), w_in[..., o_a:o_z], zeros(LANES - 2 * DN_HEADS)]
    w_mix = jnp.concatenate([w_in[..., :o_b], w_in[..., o_z:o_xaq]] + ba, axis=-1).astype(BF16)
    assert w_mix.shape[-1] == MIX_WIDTH
    w_mrg = w_in[..., o_xaq:].astype(BF16)
    return w_mix, w_mrg


def _mixer_tables(conv_w, a_log, dt_bias, norm_w, sinks):
    return (conv_w.astype(F32), _lane_row(a_log, DN_HEADS), _lane_row(dt_bias, DN_HEADS),
            norm_w.reshape(1, -1).astype(F32), sinks.astype(F32))


def kernel(x, mem, mem_ln_g, mem_ln_b, ln_g, ln_b, ffn1_w_gu, ffn1_w_down, w_in, dn_conv_w, dn_a_log,
           dn_dt_bias, dn_norm_w, swa_sinks, w_mem_kv, w_branch, w_out, ffn2_w_gu, ffn2_w_down):
    b, s, d = x.shape
    depth = ln_g.shape[0]
    alpha = float((2 * depth) ** 0.25)
    row = lambda v: v.reshape(1, -1).astype(F32)

    w_mix, w_mrg = _pack_input_projection(w_in)
    ffn1_gu, ffn1_down = ffn1_w_gu.astype(BF16), ffn1_w_down.astype(BF16)
    ffn2_gu, ffn2_down = ffn2_w_gu.astype(BF16), ffn2_w_down.astype(BF16)
    w_mem_kv_b, w_branch_b, w_out_b = w_mem_kv.astype(BF16), w_branch.astype(BF16), w_out.astype(BF16)

    for l in range(depth):
        h = _ffn_ln(x.reshape(b * s, d), ffn1_gu, ffn1_down, l, row(ln_g[l, 0]), row(ln_b[l, 0]), alpha)
        h = h.reshape(b, s, d)
        odn, osw = _mixers(h, w_mix, l, *_mixer_tables(dn_conv_w[l], dn_a_log[l], dn_dt_bias[l],
                                                       dn_norm_w[l], swa_sinks[l]))
        h = _merge(h, odn, osw, mem, row(mem_ln_g), row(mem_ln_b), w_mem_kv_b, w_mrg, w_branch_b,
                   w_out_b, l, row(ln_g[l, 1]), row(ln_b[l, 1]), alpha)
        x = _ffn_ln(h.reshape(b * s, d), ffn2_gu, ffn2_down, l, row(ln_g[l, 2]), row(ln_b[l, 2]), alpha)
        x = x.reshape(b, s, d)
    return x
```

```python
import functools

import jax
import jax.numpy as jnp
from jax import lax
from jax.experimental import pallas as pl
from jax.experimental.pallas import tpu as pltpu

F32 = jnp.float32
BF16 = jnp.bfloat16

DN_HEADS, DN_DK, DN_DV, DN_CONV, DN_CHUNK = 4, 128, 128, 4, 64
SWA_HEADS, SWA_KV_HEADS, SWA_DH, WINDOW = 8, 2, 64, 128
XA_HEADS, XA_DH = 4, 128
N_BRANCH, BRANCH_W = 3, 512
LN_EPS, RMS_EPS, NEG_INF = 1e-5, 1e-6, -1e30
MACARON_SCALE = 0.5

LANES = 128
SUBLANES = 8
VMEM_LIMIT_BYTES = 56 * 1024 * 1024

TOKEN_TILE = 512
MERGE_TOKEN_TILE = 1024
FFN_TOKEN_TILE = 1024
FF_CHUNK = 256
MERGE_CHUNK = 256
WY_GROUP = 8

_N_DN = DN_HEADS * DN_DK
MIX_QKV = slice(0, 3 * _N_DN)
MIX_Z = slice(MIX_QKV.stop, MIX_QKV.stop + DN_HEADS * DN_DV)
MIX_SWQ = slice(MIX_Z.stop, MIX_Z.stop + SWA_HEADS * SWA_DH)
MIX_SWKV = slice(MIX_SWQ.stop, MIX_SWQ.stop + 2 * SWA_KV_HEADS * SWA_DH)
MIX_BA = slice(MIX_SWKV.stop, MIX_SWKV.stop + 2 * LANES)
MIX_WIDTH = MIX_BA.stop
MRG_XAQ = slice(0, XA_HEADS * XA_DH)
MRG_GATE0 = MRG_XAQ.stop


def _dot(a, b):
    return jnp.dot(a, b, preferred_element_type=F32)


def _dot_nt(a, b):
    return lax.dot_general(a, b, (((1,), (1,)), ((), ())), preferred_element_type=F32)


def _dot_tn(a, b):
    return lax.dot_general(a, b, (((0,), (0,)), ((), ())), preferred_element_type=F32)


def _layer_norm(y, g, b):
    mu = jnp.mean(y, axis=-1, keepdims=True)
    d = y - mu
    var = jnp.mean(d * d, axis=-1, keepdims=True)
    return d * lax.rsqrt(var + LN_EPS) * g + b


def _sigmoid(x):
    return 0.5 * (jnp.tanh(0.5 * x) + 1.0)


def _silu(x):
    half = 0.5 * x
    return half * (jnp.tanh(half) + 1.0)


def _ffn_ln_kernel(x_ref, wgu_ref, wd_ref, g_ref, b_ref, o_ref, acc_ref, *, alpha):
    d_ff = wd_ref.shape[0]
    xf = x_ref[...]
    xb = xf.astype(BF16)
    for c in range(d_ff // FF_CHUNK):
        lo = c * FF_CHUNK
        gate = _dot(xb, wgu_ref[:, lo:lo + FF_CHUNK])
        up = _dot(xb, wgu_ref[:, d_ff + lo:d_ff + lo + FF_CHUNK])
        act = (_silu(gate) * up).astype(BF16)
        part = _dot(act, wd_ref[lo:lo + FF_CHUNK, :])
        if c == 0:
            acc_ref[...] = part
        else:
            acc_ref[...] += part
    y = alpha * xf + acc_ref[...]
    o_ref[...] = _layer_norm(y, g_ref[...], b_ref[...])


def _ffn_ln(x2d, w_gu, w_down, layer, g, b, alpha):
    t, d = x2d.shape
    _, d_ff, _ = w_down.shape
    tm = min(FFN_TOKEN_TILE, t)
    return pl.pallas_call(
        functools.partial(_ffn_ln_kernel, alpha=alpha),
        out_shape=jax.ShapeDtypeStruct((t, d), F32),
        grid=(t // tm,),
        in_specs=[
            pl.BlockSpec((tm, d), lambda i: (i, 0)),
            pl.BlockSpec((None, d, 2 * d_ff), lambda i: (layer, 0, 0), pipeline_mode=pl.Buffered(1)),
            pl.BlockSpec((None, d_ff, d), lambda i: (layer, 0, 0), pipeline_mode=pl.Buffered(1)),
            pl.BlockSpec((1, d), lambda i: (0, 0)),
            pl.BlockSpec((1, d), lambda i: (0, 0)),
        ],
        out_specs=pl.BlockSpec((tm, d), lambda i: (i, 0)),
        scratch_shapes=[pltpu.VMEM((tm, d), F32)],
        compiler_params=pltpu.CompilerParams(
            dimension_semantics=("arbitrary",), vmem_limit_bytes=VMEM_LIMIT_BYTES),
        name="ffn_ln",
    )(x2d, w_gu, w_down, g, b)


def _split_bf16(x):
    hi = x.astype(BF16)
    lo = (x - hi.astype(F32)).astype(BF16)
    return hi, lo


def _split3_bf16(x):
    p1 = x.astype(BF16)
    r1 = x - p1.astype(F32)
    p2 = r1.astype(BF16)
    p3 = (r1 - p2.astype(F32)).astype(BF16)
    return p1, p2, p3


def _block_diag(y_l, n_blocks):
    c = y_l.shape[0]
    lane_block = lax.broadcasted_iota(jnp.int32, y_l.shape, 1) // c
    zero = jnp.zeros_like(y_l)
    return jnp.concatenate([jnp.where(lane_block == b, y_l, zero) for b in range(n_blocks)], axis=0)


def _blockwise_matmul(l_l, y_l, n_blocks):
    lh, ll = _split_bf16(l_l)
    yh, yl = _split_bf16(y_l)
    m = l_l.shape[0]
    top = _dot(jnp.concatenate([lh, ll], axis=0), _block_diag(yh, n_blocks))
    return top[:m] + top[m:] + _dot(lh, _block_diag(yl, n_blocks))


def _unit_lower_inverses(a_ls, n_blocks):
    c = a_ls[0].shape[0]
    row = lax.broadcasted_iota(jnp.int32, a_ls[0].shape, 0)
    col = lax.broadcasted_iota(jnp.int32, a_ls[0].shape, 1) % c
    eye = (row == col).astype(F32)
    ps = [eye - a_l for a_l in a_ls]
    ys = [_blockwise_matmul(-a_l, -a_l, n_blocks) for a_l in a_ls]
    power = 2
    while power < c:
        if 2 * power < c:
            stacked = [_blockwise_matmul(jnp.concatenate([p, y], axis=0), y, n_blocks)
                       for p, y in zip(ps, ys)]
            ps = [p + s[:c] for p, s in zip(ps, stacked)]
            ys = [s[c:] for s in stacked]
        else:
            ps = [p + _blockwise_matmul(p, y, n_blocks) for p, y in zip(ps, ys)]
        power *= 2
    return ps


def _mixers_kernel(h_ref, w_ref, convw_ref, alog_ref,
                   dtb_ref, normw_ref, sinks_ref, odn_ref, osw_ref,
                   xbuf, q_s, k_s, kb_s, qd_s, kt_s, rhs_s, z_s, gcum_s, gcol_s, elast_s, o0_s, pq_s, n_s, state, kvbuf, qlo_s, qhi_s,
                   *, tm):
    j = pl.program_id(1)
    n_qkv = 3 * DN_HEADS * DN_DK
    n_dn = DN_HEADS * DN_DK
    halo = SUBLANES
    c_ = DN_CHUNK
    n_chunks = tm // c_
    gate_lane = DN_HEADS

    @pl.when(j == 0)
    def _reset():
        xbuf[0:halo, :] = jnp.zeros((halo, n_qkv), F32)
        state[...] = jnp.zeros_like(state)
        kvbuf[0:WINDOW, :] = jnp.zeros((WINDOW, kvbuf.shape[1]), BF16)

    hb = h_ref[0].astype(BF16)

    pre = _dot(hb, w_ref[:, MIX_QKV])
    xbuf[halo:halo + tm, :] = pre
    cw = convw_ref[...]
    conv = cw[DN_CONV - 1:DN_CONV, :] * pre
    for s in range(1, DN_CONV):
        conv = conv + cw[DN_CONV - 1 - s:DN_CONV - s, :] * xbuf[halo - s:halo - s + tm, :]
    xbuf[0:halo, :] = xbuf[tm:tm + halo, :]
    qkv = _silu(conv)
    z_s[...] = _silu(_dot(hb, w_ref[:, MIX_Z]))

    ba = _dot(hb, w_ref[:, MIX_BA])
    beta = _sigmoid(ba[:, :LANES])
    g = -jnp.exp(alog_ref[...]) * jax.nn.softplus(ba[:, LANES:] + dtb_ref[...])
    row_c = lax.broadcasted_iota(jnp.int32, (c_, c_), 0)
    col_c = lax.broadcasted_iota(jnp.int32, (c_, c_), 1)
    tril_b = (row_c >= col_c).astype(BF16)
    g_parts = _split3_bf16(g)
    gcum_parts, etail_parts, elast_parts = [], [], []
    for ci in range(n_chunks):
        gc = sum(_dot(tril_b, part[ci * c_:(ci + 1) * c_, :]) for part in g_parts)
        glast = gc[c_ - 1:c_, :]
        gcum_parts.append(gc)
        etail_parts.append(jnp.exp(glast - gc))
        elast_parts.append(jnp.broadcast_to(jnp.exp(glast), (c_, LANES)))
    gcum = jnp.concatenate(gcum_parts, axis=0)
    gcum_s[...] = gcum
    elast_s[...] = jnp.concatenate(elast_parts, axis=0)
    sel_k = lax.broadcasted_iota(jnp.int32, (LANES, DN_HEADS * c_), 0)
    sel_n = lax.broadcasted_iota(jnp.int32, (LANES, DN_HEADS * c_), 1)
    sel_b = (sel_k == gate_lane + sel_n // c_).astype(BF16)
    gcol_s[...] = sum(_dot(part, sel_b) for part in _split3_bf16(gcum))
    e_g = jnp.exp(gcum)
    e_tail = jnp.concatenate(etail_parts, axis=0)
    for hd in range(DN_HEADS):
        lo = hd * DN_DK
        hs = slice(lo, lo + DN_DK)
        qh = qkv[:, lo:lo + DN_DK]
        qh = qh * (lax.rsqrt(jnp.sum(qh * qh, -1, keepdims=True) + RMS_EPS) * (DN_DK ** -0.5))
        kh = qkv[:, n_dn + lo:n_dn + lo + DN_DK]
        kh = kh * lax.rsqrt(jnp.sum(kh * kh, -1, keepdims=True) + RMS_EPS)
        vh = qkv[:, 2 * n_dn + lo:2 * n_dn + lo + DN_DV]
        gl = gate_lane + hd
        beta_b = jnp.broadcast_to(beta[:, gl:gl + 1], (tm, DN_DK))
        eg_b = jnp.broadcast_to(e_g[:, gl:gl + 1], (tm, DN_DK))
        et_b = jnp.broadcast_to(e_tail[:, gl:gl + 1], (tm, DN_DK))
        kb = kh * beta_b
        q_s[:, hs] = qh.astype(BF16)
        k_s[:, hs] = kh.astype(BF16)
        kb_s[:, hs] = kb.astype(BF16)
        qd_s[:, hs] = (qh * eg_b).astype(BF16)
        kt_s[:, hs] = (kh * et_b).astype(BF16)
        rhs_s[:, 2 * lo:2 * lo + DN_DV] = (vh * beta_b).astype(BF16)
        rhs_s[:, 2 * lo + DN_DV:2 * lo + DN_DV + DN_DK] = (kb * eg_b).astype(BF16)

    stack_w = DN_HEADS * c_
    row_l = lax.broadcasted_iota(jnp.int32, (c_, stack_w), 0)
    col_l = lax.broadcasted_iota(jnp.int32, (c_, stack_w), 1) % c_
    tril_l = row_l >= col_l
    strict_l = row_l > col_l
    lane = lax.broadcasted_iota(jnp.int32, (c_, LANES), 1)
    head_block = lax.broadcasted_iota(jnp.int32, (c_, n_dn), 1) // DN_DK
    ones_l = jnp.ones((c_, LANES), BF16)
    normw = normw_ref[...]
    zero_rhs = jnp.zeros((c_, DN_DV + DN_DK), BF16)
    heads = range(DN_HEADS)
    hsl = [slice(hd * DN_DK, (hd + 1) * DN_DK) for hd in heads]
    pairs = [slice((hd // 2) * LANES, (hd // 2 + 1) * LANES) for hd in heads]

    def wy_group_body(gi, carry):
        g0 = pl.multiple_of(gi * (WY_GROUP * c_), WY_GROUP * c_)
        chunk_rows = [pl.ds(g0 + ci * c_, c_) for ci in range(WY_GROUP)]
        decays = []
        for rows in chunk_rows:
            gc = gcum_s[rows, :]
            g_rows = jnp.concatenate(
                [jnp.where(lane == gate_lane + hd, gc, 0.0) for hd in range(DN_HEADS)], axis=0)
            grow_l = sum(_dot_nt(ones_l, part) for part in _split3_bf16(g_rows))
            decays.append(jnp.exp(jnp.where(tril_l, gcol_s[rows, :] - grow_l, NEG_INF)))
        kqs = []
        for rows in chunk_rows:
            k_rows = k_s[rows, :]
            k_bd = jnp.concatenate(
                [jnp.where(head_block == hd, k_rows, jnp.zeros_like(k_rows)) for hd in range(DN_HEADS)],
                axis=0)
            kqs.append(_dot_nt(jnp.concatenate([kb_s[rows, :], q_s[rows, :]], axis=0), k_bd))
        a_ls = [jnp.where(strict_l, kq[:c_] * dec, 0.0) for kq, dec in zip(kqs, decays)]
        qk_bs = [jnp.where(tril_l, kq[c_:] * dec, 0.0).astype(BF16) for kq, dec in zip(kqs, decays)]
        t_invs = _unit_lower_inverses(a_ls, DN_HEADS)
        sols = []
        for rows, t_inv in zip(chunk_rows, t_invs):
            t_inv_b = t_inv.astype(BF16)
            for hd in heads:
                rhs_h = rhs_s[rows, 2 * hd * DN_DK:2 * hd * DN_DK + DN_DV + DN_DK]
                rhs_pad = ([rhs_h, zero_rhs] if hd % 2 == 0 else [zero_rhs, rhs_h])
                sols.append(_dot(t_inv_b[:, pairs[hd]], jnp.concatenate(rhs_pad, axis=0)).astype(BF16))
        for n, (rows, qk_b) in enumerate(zip(chunk_rows, qk_bs)):
            for hd in heads:
                sol_b = sols[n * DN_HEADS + hd]
                sol_pad = ([sol_b, zero_rhs] if hd % 2 == 0 else [zero_rhs, sol_b])
                qk_uw = _dot(qk_b[:, pairs[hd]], jnp.concatenate(sol_pad, axis=0))
                kt_uw = _dot_tn(kt_s[rows, hsl[hd]], sol_b)
                q_eff = qd_s[rows, hsl[hd]].astype(F32) - qk_uw[:, DN_DV:]
                slot = (gi * WY_GROUP + n) * DN_HEADS + hd
                o0_s[rows, hsl[hd]] = qk_uw[:, :DN_DV]
                pq_s[slot] = jnp.concatenate([kt_uw[:, DN_DV:], q_eff], axis=0).astype(BF16)
                n_s[slot] = kt_uw[:, :DN_DV]
        return carry

    def chunk_body(ci, carry):
        r0 = pl.multiple_of(ci * c_, c_)
        rows = pl.ds(r0, c_)
        e_last = elast_s[pl.ds(r0, 1), :]
        s_old = [state[hd] for hd in heads]
        prods = [_dot(pq_s[ci * DN_HEADS + hd], s_old[hd].astype(BF16)) for hd in heads]
        for hd in heads:
            state[hd] = (s_old[hd] * e_last[:, gate_lane + hd:gate_lane + hd + 1]
                         - prods[hd][:DN_DK] + n_s[ci * DN_HEADS + hd])
        for hd in heads:
            o0_s[rows, hsl[hd]] += prods[hd][DN_DK:]
        return carry

    def deltanet_outputs():
        for hd in heads:
            o = o0_s[:, hsl[hd]]
            o = o * lax.rsqrt(jnp.mean(o * o, -1, keepdims=True) + RMS_EPS) * normw
            odn_ref[0, :, hsl[hd]] = (o * z_s[:, hsl[hd]]).astype(odn_ref.dtype)

    w_ = WINDOW
    group = SWA_HEADS // SWA_KV_HEADS
    swq = _dot(hb, w_ref[:, MIX_SWQ]) * (SWA_DH ** -0.5)
    kv = _dot(hb, w_ref[:, MIX_SWKV])
    lane_kv = lax.broadcasted_iota(jnp.int32, (tm, LANES), 1)
    first_half = lane_kv < SWA_DH
    for part in range(2):
        x = kv[:, part * LANES:(part + 1) * LANES]
        xr = pltpu.roll(x, SWA_DH, axis=1)
        dup0 = jnp.where(first_half, x, xr)
        dup1 = jnp.where(first_half, xr, x)
        kvbuf[w_:w_ + tm, (2 * part) * LANES:(2 * part + 1) * LANES] = dup0.astype(BF16)
        kvbuf[w_:w_ + tm, (2 * part + 1) * LANES:(2 * part + 2) * LANES] = dup1.astype(BF16)

    lane_q = lax.broadcasted_iota(jnp.int32, (tm, SWA_HEADS * SWA_DH), 1) % LANES
    qlo_s[...] = jnp.where(lane_q < SWA_DH, swq, 0.0).astype(BF16)
    qhi_s[...] = jnp.where(lane_q < SWA_DH, 0.0, swq).astype(BF16)

    qi = lax.broadcasted_iota(jnp.int32, (group * w_, 2 * w_), 0) % w_
    kj = lax.broadcasted_iota(jnp.int32, (group * w_, 2 * w_), 1)
    band = (kj > qi) & (kj <= qi + w_)
    lo_half = lax.broadcasted_iota(jnp.int32, (w_, LANES), 1) < SWA_DH
    sinks4 = [jnp.concatenate([jnp.full((w_, 1), sinks_ref[kvh * group + gi], F32)
                               for gi in range(group)], axis=0) for kvh in range(SWA_KV_HEADS)]
    kv_heads = range(SWA_KV_HEADS)

    ones_v = jnp.ones((2 * w_, LANES), BF16)

    problems = [(blk, kvh) for blk in range(tm // w_) for kvh in kv_heads]

    def swa_scores(blk, kvh):
        pieces = []
        for pair in range(group // 2):
            cols = slice((kvh * (group // 2) + pair) * LANES, (kvh * (group // 2) + pair + 1) * LANES)
            pieces += [qlo_s[blk * w_:(blk + 1) * w_, cols], qhi_s[blk * w_:(blk + 1) * w_, cols]]
        q4 = jnp.concatenate(pieces, axis=0)
        kd = kvbuf[blk * w_:(blk + 2) * w_, kvh * LANES:(kvh + 1) * LANES]
        mask = band & jnp.logical_or(kj >= w_, j > 0) if blk == 0 else band
        return jnp.where(mask, _dot_nt(q4, kd), NEG_INF)

    def swa_outputs():
        scores = [swa_scores(blk, kvh) for blk, kvh in problems]
        maxes = [jnp.maximum(jnp.max(s, axis=-1, keepdims=True), sinks4[kvh])
                 for s, (blk, kvh) in zip(scores, problems)]
        probs = [jnp.exp(s - m).astype(BF16) for s, m in zip(scores, maxes)]
        for p, m, (blk, kvh) in zip(probs, maxes, problems):
            vd = kvbuf[blk * w_:(blk + 2) * w_, (2 + kvh) * LANES:(3 + kvh) * LANES]
            pv = _dot(p, jnp.concatenate([vd, ones_v], axis=1))
            o4 = pv[:, :LANES] / (pv[:, LANES:] + jnp.exp(sinks4[kvh] - m))
            for pair in range(group // 2):
                cols = slice((kvh * (group // 2) + pair) * LANES, (kvh * (group // 2) + pair + 1) * LANES)
                o_pair = jnp.where(lo_half, o4[(2 * pair) * w_:(2 * pair + 1) * w_],
                                   o4[(2 * pair + 1) * w_:(2 * pair + 2) * w_])
                osw_ref[0, blk * w_:(blk + 1) * w_, cols] = o_pair.astype(osw_ref.dtype)

    lax.fori_loop(0, n_chunks // WY_GROUP, wy_group_body, 0)
    lax.fori_loop(0, n_chunks, chunk_body, 0)
    deltanet_outputs()
    swa_outputs()
    kvbuf[0:w_, :] = kvbuf[tm:tm + w_, :]


def _mixers(h, w_mix, layer, convw, alog_row, dtb_row, normw_row, sinks):
    b, s, d = h.shape
    tm = min(TOKEN_TILE, s)
    n_qkv = 3 * DN_HEADS * DN_DK
    dn_w = DN_HEADS * DN_DV
    sw_w = SWA_HEADS * SWA_DH
    n_slots = (tm // DN_CHUNK) * DN_HEADS
    full = lambda shape: pl.BlockSpec(shape, lambda bi, ji: tuple(0 for _ in shape))
    tok = lambda width: pl.BlockSpec((1, tm, width), lambda bi, ji: (bi, ji, 0))
    return pl.pallas_call(
        functools.partial(_mixers_kernel, tm=tm),
        out_shape=(jax.ShapeDtypeStruct((b, s, dn_w), BF16), jax.ShapeDtypeStruct((b, s, sw_w), BF16)),
        grid=(b, s // tm),
        in_specs=[
            tok(d),
            pl.BlockSpec((None, d, w_mix.shape[-1]), lambda bi, ji: (layer, 0, 0)),
            full(convw.shape), full(alog_row.shape), full(dtb_row.shape), full(normw_row.shape),
            pl.BlockSpec(memory_space=pltpu.SMEM),
        ],
        out_specs=(tok(dn_w), tok(sw_w)),
        scratch_shapes=[
            pltpu.VMEM((tm + SUBLANES, n_qkv), F32),
            pltpu.VMEM((tm, dn_w), BF16),
            pltpu.VMEM((tm, dn_w), BF16),
            pltpu.VMEM((tm, dn_w), BF16),
            pltpu.VMEM((tm, dn_w), BF16),
            pltpu.VMEM((tm, dn_w), BF16),
            pltpu.VMEM((tm, 2 * dn_w), BF16),
            pltpu.VMEM((tm, dn_w), F32),
            pltpu.VMEM((tm, LANES), F32),
            pltpu.VMEM((tm, DN_HEADS * DN_CHUNK), F32),
            pltpu.VMEM((tm, LANES), F32),
            pltpu.VMEM((tm, dn_w), F32),
            pltpu.VMEM((n_slots, DN_DK + DN_CHUNK, DN_DK), BF16),
            pltpu.VMEM((n_slots, DN_DK, DN_DV), F32),
            pltpu.VMEM((DN_HEADS, DN_DK, DN_DV), F32),
            pltpu.VMEM((tm + WINDOW, 4 * LANES), BF16),
            pltpu.VMEM((tm, sw_w), BF16),
            pltpu.VMEM((tm, sw_w), BF16),
        ],
        compiler_params=pltpu.CompilerParams(
            dimension_semantics=("arbitrary", "arbitrary"), vmem_limit_bytes=VMEM_LIMIT_BYTES),
        name="mixers",
    )(h, w_mix, convw, alog_row, dtb_row, normw_row, sinks)


def _merge_kernel(h_ref, odn_ref, osw_ref, mem_ref, memg_ref, memb_ref, wmkv_ref, w_ref,
                  wb_ref, wo_ref, g_ref, b_ref, o_ref, kmem, vmem, merged, *, alpha):
    j = pl.program_id(1)
    xa_w = XA_HEADS * XA_DH

    @pl.when(j == 0)
    def _memory_kv():
        mem_n = _layer_norm(mem_ref[0], memg_ref[...], memb_ref[...])
        kvm = _dot(mem_n.astype(BF16), wmkv_ref[...])
        kmem[...] = kvm[:, :xa_w].astype(BF16)
        vmem[...] = kvm[:, xa_w:].astype(BF16)

    hf = h_ref[0]
    hb = hf.astype(BF16)
    xq = _dot(hb, w_ref[:, MRG_XAQ]) * (XA_DH ** -0.5)
    hsl = [slice(hd * XA_DH, (hd + 1) * XA_DH) for hd in range(XA_HEADS)]
    xqb = xq.astype(BF16)
    scores = [_dot_nt(xqb[:, sl], kmem[:, sl]) for sl in hsl]
    probs = [jnp.exp(s - jnp.max(s, axis=-1, keepdims=True)) for s in scores]
    denoms = [jnp.sum(p, axis=-1, keepdims=True) for p in probs]
    heads = [_dot(p.astype(BF16), vmem[:, sl]) / dn for p, dn, sl in zip(probs, denoms, hsl)]
    oxa = jnp.concatenate(heads, axis=1).astype(BF16)
    branches = (odn_ref[0], osw_ref[0], oxa)
    d = hf.shape[-1]
    for dc in range(d // MERGE_CHUNK):
        lo = dc * MERGE_CHUNK
        acc = None
        for n in range(N_BRANCH):
            g0 = MRG_GATE0 + n * d + lo
            gate = _sigmoid(_dot(hb, w_ref[:, g0:g0 + MERGE_CHUNK]))
            term = gate * _dot(branches[n], wb_ref[n, :, lo:lo + MERGE_CHUNK])
            acc = term if acc is None else acc + term
        merged[:, lo:lo + MERGE_CHUNK] = acc.astype(BF16)
    y = alpha * hf + _dot(merged[...], wo_ref[...])
    o_ref[0] = _layer_norm(y, g_ref[...], b_ref[...])


def _merge(h, odn, osw, mem, memg, memb, wmkv, w_mrg, wbranch, wout, layer, g, b_, alpha):
    b, s, d = h.shape
    tm = min(MERGE_TOKEN_TILE, s)
    n_mem = mem.shape[1]
    xa_w = XA_HEADS * XA_DH
    full = lambda shape: pl.BlockSpec(shape, lambda bi, ji: tuple(0 for _ in shape))
    tok = lambda width: pl.BlockSpec((1, tm, width), lambda bi, ji: (bi, ji, 0))
    layer_block = lambda w: pl.BlockSpec((None,) + w.shape[1:],
                                         lambda bi, ji: (layer,) + (0,) * (w.ndim - 1),
                                         pipeline_mode=pl.Buffered(1))
    return pl.pallas_call(
        functools.partial(_merge_kernel, alpha=alpha),
        out_shape=jax.ShapeDtypeStruct((b, s, d), F32),
        grid=(b, s // tm),
        in_specs=[
            tok(d), tok(odn.shape[-1]), tok(osw.shape[-1]),
            pl.BlockSpec((1, n_mem, d), lambda bi, ji: (bi, 0, 0)),
            full(memg.shape), full(memb.shape), layer_block(wmkv), layer_block(w_mrg),
            layer_block(wbranch), layer_block(wout), full(g.shape), full(b_.shape),
        ],
        out_specs=tok(d),
        scratch_shapes=[
            pltpu.VMEM((n_mem, xa_w), BF16),
            pltpu.VMEM((n_mem, xa_w), BF16),
            pltpu.VMEM((tm, d), BF16),
        ],
        compiler_params=pltpu.CompilerParams(
            dimension_semantics=("arbitrary", "arbitrary"), vmem_limit_bytes=VMEM_LIMIT_BYTES),
        name="merge",
    )(h, odn, osw, mem, memg, memb, wmkv, w_mrg, wbranch, wout, g, b_)


def _lane_row(vals, offset):
    return jnp.zeros((1, LANES), F32).at[0, offset:offset + vals.shape[0]].set(vals.astype(F32))


def _pack_input_projection(w_in):
    depth, d, _ = w_in.shape
    o_b = 3 * _N_DN
    o_a = o_b + DN_HEADS
    o_z = o_a + DN_HEADS
    o_swq = o_z + DN_HEADS * DN_DV
    o_swk = o_swq + SWA_HEADS * SWA_DH
    o_xaq = o_swk + 2 * SWA_KV_HEADS * SWA_DH
    zeros = lambda n: jnp.zeros((depth, d, n), w_in.dtype)
    ba = [zeros(DN_HEADS), w_in[..., o_b:o_a], zeros(LANES - 2 * DN_HEADS),
          zeros(DN_HEADS), w_in[..., o_a:o_z], zeros(LANES - 2 * DN_HEADS)]
    w_mix = jnp.concatenate([w_in[..., :o_b], w_in[..., o_z:o_xaq]] + ba, axis=-1).astype(BF16)
    assert w_mix.shape[-1] == MIX_WIDTH
    w_mrg = w_in[..., o_xaq:].astype(BF16)
    return w_mix, w_mrg


def _mixer_tables(conv_w, a_log, dt_bias, norm_w, sinks):
    return (conv_w.astype(F32), _lane_row(a_log, DN_HEADS), _lane_row(dt_bias, DN_HEADS),
            norm_w.reshape(1, -1).astype(F32), sinks.astype(F32))


def kernel(x, mem, mem_ln_g, mem_ln_b, ln_g, ln_b, ffn1_w_gu, ffn1_w_down, w_in, dn_conv_w, dn_a_log,
           dn_dt_bias, dn_norm_w, swa_sinks, w_mem_kv, w_branch, w_out, ffn2_w_gu, ffn2_w_down):
    b, s, d = x.shape
    depth = ln_g.shape[0]
    alpha = float((2 * depth) ** 0.25)
    row = lambda v: v.reshape(1, -1).astype(F32)

    w_mix, w_mrg = _pack_input_projection(w_in)
    ffn1_gu, ffn1_down = ffn1_w_gu.astype(BF16), (MACARON_SCALE * ffn1_w_down).astype(BF16)
    ffn2_gu, ffn2_down = ffn2_w_gu.astype(BF16), (MACARON_SCALE * ffn2_w_down).astype(BF16)
    w_mem_kv_b, w_branch_b, w_out_b = w_mem_kv.astype(BF16), w_branch.astype(BF16), w_out.astype(BF16)

    for l in range(depth):
        h = _ffn_ln(x.reshape(b * s, d), ffn1_gu, ffn1_down, l, row(ln_g[l, 0]), row(ln_b[l, 0]), alpha)
        h = h.reshape(b, s, d)
        odn, osw = _mixers(h, w_mix, l, *_mixer_tables(dn_conv_w[l], dn_a_log[l], dn_dt_bias[l],
                                                       dn_norm_w[l], swa_sinks[l]))
        h = _merge(h, odn, osw, mem, row(mem_ln_g), row(mem_ln_b), w_mem_kv_b, w_mrg, w_branch_b,
                   w_out_b, l, row(ln_g[l, 1]), row(ln_b[l, 1]), alpha)
        x = _ffn_ln(h.reshape(b * s, d), ffn2_gu, ffn2_down, l, row(ln_g[l, 2]), row(ln_b[l, 2]), alpha)
        x = x.reshape(b, s, d)
    return x
```

```python
import functools

import jax
import jax.numpy as jnp
from jax import lax
from jax.experimental import pallas as pl
from jax.experimental.pallas import tpu as pltpu

F32 = jnp.float32
BF16 = jnp.bfloat16

DN_HEADS, DN_DK, DN_DV, DN_CONV, DN_CHUNK = 4, 128, 128, 4, 64
SWA_HEADS, SWA_KV_HEADS, SWA_DH, WINDOW = 8, 2, 64, 128
XA_HEADS, XA_DH = 4, 128
N_BRANCH, BRANCH_W = 3, 512
LN_EPS, RMS_EPS, NEG_INF = 1e-5, 1e-6, -1e30
MACARON_SCALE = 0.5

LANES = 128
SUBLANES = 8
VMEM_LIMIT_BYTES = 56 * 1024 * 1024

TOKEN_TILE = 512
MERGE_TOKEN_TILE = 1024
FFN_TOKEN_TILE = 1024
FF_CHUNK = 256
MERGE_CHUNK = 256
PACK_ROWS = 256
WY_GROUP = 8

_N_DN = DN_HEADS * DN_DK
MIX_QKV = slice(0, 3 * _N_DN)
MIX_Z = slice(MIX_QKV.stop, MIX_QKV.stop + DN_HEADS * DN_DV)
MIX_SWQ = slice(MIX_Z.stop, MIX_Z.stop + SWA_HEADS * SWA_DH)
MIX_SWKV = slice(MIX_SWQ.stop, MIX_SWQ.stop + 2 * SWA_KV_HEADS * SWA_DH)
MIX_BA = slice(MIX_SWKV.stop, MIX_SWKV.stop + 2 * LANES)
MIX_WIDTH = MIX_BA.stop
MRG_XAQ = slice(0, XA_HEADS * XA_DH)
MRG_GATE0 = MRG_XAQ.stop


def _dot(a, b):
    return jnp.dot(a, b, preferred_element_type=F32)


def _dot_nt(a, b):
    return lax.dot_general(a, b, (((1,), (1,)), ((), ())), preferred_element_type=F32)


def _dot_tn(a, b):
    return lax.dot_general(a, b, (((0,), (0,)), ((), ())), preferred_element_type=F32)


def _layer_norm(y, g, b):
    mu = jnp.mean(y, axis=-1, keepdims=True)
    d = y - mu
    var = jnp.mean(d * d, axis=-1, keepdims=True)
    return d * lax.rsqrt(var + LN_EPS) * g + b


def _sigmoid(x):
    return 0.5 * (jnp.tanh(0.5 * x) + 1.0)


def _silu(x):
    half = 0.5 * x
    return half * (jnp.tanh(half) + 1.0)


def _ffn_ln_kernel(x_ref, wgu_ref, wd_ref, g_ref, b_ref, o_ref, acc_ref, *, alpha):
    d_ff = wd_ref.shape[0]
    xf = x_ref[...]
    xb = xf.astype(BF16)
    for c in range(d_ff // FF_CHUNK):
        lo = c * FF_CHUNK
        gate = _dot(xb, wgu_ref[:, lo:lo + FF_CHUNK])
        up = _dot(xb, wgu_ref[:, d_ff + lo:d_ff + lo + FF_CHUNK])
        act = (_silu(gate) * up).astype(BF16)
        part = _dot(act, wd_ref[lo:lo + FF_CHUNK, :])
        if c == 0:
            acc_ref[...] = part
        else:
            acc_ref[...] += part
    y = alpha * xf + acc_ref[...]
    o_ref[...] = _layer_norm(y, g_ref[...], b_ref[...])


def _ffn_ln(x2d, w_gu, w_down, layer, g, b, alpha):
    t, d = x2d.shape
    _, d_ff, _ = w_down.shape
    tm = min(FFN_TOKEN_TILE, t)
    return pl.pallas_call(
        functools.partial(_ffn_ln_kernel, alpha=alpha),
        out_shape=jax.ShapeDtypeStruct((t, d), F32),
        grid=(t // tm,),
        in_specs=[
            pl.BlockSpec((tm, d), lambda i: (i, 0)),
            pl.BlockSpec((None, d, 2 * d_ff), lambda i: (layer, 0, 0), pipeline_mode=pl.Buffered(1)),
            pl.BlockSpec((None, d_ff, d), lambda i: (layer, 0, 0), pipeline_mode=pl.Buffered(1)),
            pl.BlockSpec((1, d), lambda i: (0, 0)),
            pl.BlockSpec((1, d), lambda i: (0, 0)),
        ],
        out_specs=pl.BlockSpec((tm, d), lambda i: (i, 0)),
        scratch_shapes=[pltpu.VMEM((tm, d), F32)],
        compiler_params=pltpu.CompilerParams(
            dimension_semantics=("arbitrary",), vmem_limit_bytes=VMEM_LIMIT_BYTES),
        name="ffn_ln",
    )(x2d, w_gu, w_down, g, b)


def _split_bf16(x):
    hi = x.astype(BF16)
    lo = (x - hi.astype(F32)).astype(BF16)
    return hi, lo


def _split3_bf16(x):
    p1 = x.astype(BF16)
    r1 = x - p1.astype(F32)
    p2 = r1.astype(BF16)
    p3 = (r1 - p2.astype(F32)).astype(BF16)
    return p1, p2, p3


def _block_diag(y_l, n_blocks):
    c = y_l.shape[0]
    lane_block = lax.broadcasted_iota(jnp.int32, y_l.shape, 1) // c
    zero = jnp.zeros_like(y_l)
    return jnp.concatenate([jnp.where(lane_block == b, y_l, zero) for b in range(n_blocks)], axis=0)


def _blockwise_matmul(l_l, y_l, n_blocks):
    lh, ll = _split_bf16(l_l)
    yh, yl = _split_bf16(y_l)
    m = l_l.shape[0]
    top = _dot(jnp.concatenate([lh, ll], axis=0), _block_diag(yh, n_blocks))
    return top[:m] + top[m:] + _dot(lh, _block_diag(yl, n_blocks))


def _unit_lower_inverses(a_ls, n_blocks):
    c = a_ls[0].shape[0]
    row = lax.broadcasted_iota(jnp.int32, a_ls[0].shape, 0)
    col = lax.broadcasted_iota(jnp.int32, a_ls[0].shape, 1) % c
    eye = (row == col).astype(F32)
    ps = [eye - a_l for a_l in a_ls]
    ys = [_blockwise_matmul(-a_l, -a_l, n_blocks) for a_l in a_ls]
    power = 2
    while power < c:
        if 2 * power < c:
            stacked = [_blockwise_matmul(jnp.concatenate([p, y], axis=0), y, n_blocks)
                       for p, y in zip(ps, ys)]
            ps = [p + s[:c] for p, s in zip(ps, stacked)]
            ys = [s[c:] for s in stacked]
        else:
            ps = [p + _blockwise_matmul(p, y, n_blocks) for p, y in zip(ps, ys)]
        power *= 2
    return ps


def _mixers_kernel(h_ref, w_ref, convw_ref, alog_ref,
                   dtb_ref, normw_ref, sinks_ref, odn_ref, osw_ref,
                   xbuf, q_s, k_s, kb_s, qd_s, kt_s, rhs_s, z_s, gcum_s, gcol_s, elast_s, o0_s, pq_s, n_s, state, kvbuf, qlo_s, qhi_s,
                   *, tm):
    j = pl.program_id(1)
    n_qkv = 3 * DN_HEADS * DN_DK
    n_dn = DN_HEADS * DN_DK
    halo = SUBLANES
    c_ = DN_CHUNK
    n_chunks = tm // c_
    gate_lane = DN_HEADS

    @pl.when(j == 0)
    def _reset():
        xbuf[0:halo, :] = jnp.zeros((halo, n_qkv), F32)
        state[...] = jnp.zeros_like(state)
        kvbuf[0:WINDOW, :] = jnp.zeros((WINDOW, kvbuf.shape[1]), BF16)

    hb = h_ref[0].astype(BF16)

    pre = _dot(hb, w_ref[:, MIX_QKV])
    xbuf[halo:halo + tm, :] = pre
    cw = convw_ref[...]
    conv = cw[DN_CONV - 1:DN_CONV, :] * pre
    for s in range(1, DN_CONV):
        conv = conv + cw[DN_CONV - 1 - s:DN_CONV - s, :] * xbuf[halo - s:halo - s + tm, :]
    xbuf[0:halo, :] = xbuf[tm:tm + halo, :]
    qkv = _silu(conv)
    z_s[...] = _silu(_dot(hb, w_ref[:, MIX_Z]))

    ba = _dot(hb, w_ref[:, MIX_BA])
    beta = _sigmoid(ba[:, :LANES])
    g = -jnp.exp(alog_ref[...]) * jax.nn.softplus(ba[:, LANES:] + dtb_ref[...])
    row_c = lax.broadcasted_iota(jnp.int32, (c_, c_), 0)
    col_c = lax.broadcasted_iota(jnp.int32, (c_, c_), 1)
    tril_b = (row_c >= col_c).astype(BF16)
    g_parts = _split3_bf16(g)
    gcum_parts, etail_parts, elast_parts = [], [], []
    for ci in range(n_chunks):
        gc = sum(_dot(tril_b, part[ci * c_:(ci + 1) * c_, :]) for part in g_parts)
        glast = gc[c_ - 1:c_, :]
        gcum_parts.append(gc)
        etail_parts.append(jnp.exp(glast - gc))
        elast_parts.append(jnp.broadcast_to(jnp.exp(glast), (c_, LANES)))
    gcum = jnp.concatenate(gcum_parts, axis=0)
    gcum_s[...] = gcum
    elast_s[...] = jnp.concatenate(elast_parts, axis=0)
    sel_k = lax.broadcasted_iota(jnp.int32, (LANES, DN_HEADS * c_), 0)
    sel_n = lax.broadcasted_iota(jnp.int32, (LANES, DN_HEADS * c_), 1)
    sel_b = (sel_k == gate_lane + sel_n // c_).astype(BF16)
    gcol_s[...] = sum(_dot(part, sel_b) for part in _split3_bf16(gcum))
    e_g = jnp.exp(gcum)
    e_tail = jnp.concatenate(etail_parts, axis=0)
    for hd in range(DN_HEADS):
        lo = hd * DN_DK
        hs = slice(lo, lo + DN_DK)
        qh = qkv[:, lo:lo + DN_DK]
        qh = qh * (lax.rsqrt(jnp.sum(qh * qh, -1, keepdims=True) + RMS_EPS) * (DN_DK ** -0.5))
        kh = qkv[:, n_dn + lo:n_dn + lo + DN_DK]
        kh = kh * lax.rsqrt(jnp.sum(kh * kh, -1, keepdims=True) + RMS_EPS)
        vh = qkv[:, 2 * n_dn + lo:2 * n_dn + lo + DN_DV]
        gl = gate_lane + hd
        beta_b = jnp.broadcast_to(beta[:, gl:gl + 1], (tm, DN_DK))
        eg_b = jnp.broadcast_to(e_g[:, gl:gl + 1], (tm, DN_DK))
        et_b = jnp.broadcast_to(e_tail[:, gl:gl + 1], (tm, DN_DK))
        kb = kh * beta_b
        q_s[:, hs] = qh.astype(BF16)
        k_s[:, hs] = kh.astype(BF16)
        kb_s[:, hs] = kb.astype(BF16)
        qd_s[:, hs] = (qh * eg_b).astype(BF16)
        kt_s[:, hs] = (kh * et_b).astype(BF16)
        rhs_s[:, 2 * lo:2 * lo + DN_DV] = (vh * beta_b).astype(BF16)
        rhs_s[:, 2 * lo + DN_DV:2 * lo + DN_DV + DN_DK] = (kb * eg_b).astype(BF16)

    stack_w = DN_HEADS * c_
    row_l = lax.broadcasted_iota(jnp.int32, (c_, stack_w), 0)
    col_l = lax.broadcasted_iota(jnp.int32, (c_, stack_w), 1) % c_
    tril_l = row_l >= col_l
    strict_l = row_l > col_l
    lane = lax.broadcasted_iota(jnp.int32, (c_, LANES), 1)
    head_block = lax.broadcasted_iota(jnp.int32, (c_, n_dn), 1) // DN_DK
    ones_l = jnp.ones((c_, LANES), BF16)
    normw = normw_ref[...]
    zero_rhs = jnp.zeros((c_, DN_DV + DN_DK), BF16)
    heads = range(DN_HEADS)
    hsl = [slice(hd * DN_DK, (hd + 1) * DN_DK) for hd in heads]
    pairs = [slice((hd // 2) * LANES, (hd // 2 + 1) * LANES) for hd in heads]

    def wy_group_body(gi, carry):
        g0 = pl.multiple_of(gi * (WY_GROUP * c_), WY_GROUP * c_)
        chunk_rows = [pl.ds(g0 + ci * c_, c_) for ci in range(WY_GROUP)]
        decays = []
        for rows in chunk_rows:
            gc = gcum_s[rows, :]
            g_rows = jnp.concatenate(
                [jnp.where(lane == gate_lane + hd, gc, 0.0) for hd in range(DN_HEADS)], axis=0)
            grow_l = sum(_dot_nt(ones_l, part) for part in _split3_bf16(g_rows))
            decays.append(jnp.exp(jnp.where(tril_l, gcol_s[rows, :] - grow_l, NEG_INF)))
        kqs = []
        for rows in chunk_rows:
            k_rows = k_s[rows, :]
            k_bd = jnp.concatenate(
                [jnp.where(head_block == hd, k_rows, jnp.zeros_like(k_rows)) for hd in range(DN_HEADS)],
                axis=0)
            kqs.append(_dot_nt(jnp.concatenate([kb_s[rows, :], q_s[rows, :]], axis=0), k_bd))
        a_ls = [jnp.where(strict_l, kq[:c_] * dec, 0.0) for kq, dec in zip(kqs, decays)]
        qk_bs = [jnp.where(tril_l, kq[c_:] * dec, 0.0).astype(BF16) for kq, dec in zip(kqs, decays)]
        t_invs = _unit_lower_inverses(a_ls, DN_HEADS)
        sols = []
        for rows, t_inv in zip(chunk_rows, t_invs):
            t_inv_b = t_inv.astype(BF16)
            for hd in heads:
                rhs_h = rhs_s[rows, 2 * hd * DN_DK:2 * hd * DN_DK + DN_DV + DN_DK]
                rhs_pad = ([rhs_h, zero_rhs] if hd % 2 == 0 else [zero_rhs, rhs_h])
                sols.append(_dot(t_inv_b[:, pairs[hd]], jnp.concatenate(rhs_pad, axis=0)).astype(BF16))
        for n, (rows, qk_b) in enumerate(zip(chunk_rows, qk_bs)):
            for hd in heads:
                sol_b = sols[n * DN_HEADS + hd]
                sol_pad = ([sol_b, zero_rhs] if hd % 2 == 0 else [zero_rhs, sol_b])
                qk_uw = _dot(qk_b[:, pairs[hd]], jnp.concatenate(sol_pad, axis=0))
                kt_uw = _dot_tn(kt_s[rows, hsl[hd]], sol_b)
                q_eff = qd_s[rows, hsl[hd]].astype(F32) - qk_uw[:, DN_DV:]
                slot = (gi * WY_GROUP + n) * DN_HEADS + hd
                o0_s[rows, hsl[hd]] = qk_uw[:, :DN_DV]
                pq_s[slot] = jnp.concatenate([kt_uw[:, DN_DV:], q_eff], axis=0).astype(BF16)
                n_s[slot] = kt_uw[:, :DN_DV]
        return carry

    def chunk_body(ci, carry):
        r0 = pl.multiple_of(ci * c_, c_)
        rows = pl.ds(r0, c_)
        e_last = elast_s[pl.ds(r0, 1), :]
        s_old = [state[hd] for hd in heads]
        prods = [_dot(pq_s[ci * DN_HEADS + hd], s_old[hd].astype(BF16)) for hd in heads]
        for hd in heads:
            state[hd] = (s_old[hd] * e_last[:, gate_lane + hd:gate_lane + hd + 1]
                         - prods[hd][:DN_DK] + n_s[ci * DN_HEADS + hd])
        for hd in heads:
            o0_s[rows, hsl[hd]] += prods[hd][DN_DK:]
        return carry

    def deltanet_outputs():
        for hd in heads:
            o = o0_s[:, hsl[hd]]
            o = o * lax.rsqrt(jnp.mean(o * o, -1, keepdims=True) + RMS_EPS) * normw
            odn_ref[0, :, hsl[hd]] = (o * z_s[:, hsl[hd]]).astype(odn_ref.dtype)

    w_ = WINDOW
    group = SWA_HEADS // SWA_KV_HEADS
    swq = _dot(hb, w_ref[:, MIX_SWQ]) * (SWA_DH ** -0.5)
    kv = _dot(hb, w_ref[:, MIX_SWKV])
    lane_kv = lax.broadcasted_iota(jnp.int32, (tm, LANES), 1)
    first_half = lane_kv < SWA_DH
    for part in range(2):
        x = kv[:, part * LANES:(part + 1) * LANES]
        xr = pltpu.roll(x, SWA_DH, axis=1)
        dup0 = jnp.where(first_half, x, xr)
        dup1 = jnp.where(first_half, xr, x)
        kvbuf[w_:w_ + tm, (2 * part) * LANES:(2 * part + 1) * LANES] = dup0.astype(BF16)
        kvbuf[w_:w_ + tm, (2 * part + 1) * LANES:(2 * part + 2) * LANES] = dup1.astype(BF16)

    lane_q = lax.broadcasted_iota(jnp.int32, (tm, SWA_HEADS * SWA_DH), 1) % LANES
    qlo_s[...] = jnp.where(lane_q < SWA_DH, swq, 0.0).astype(BF16)
    qhi_s[...] = jnp.where(lane_q < SWA_DH, 0.0, swq).astype(BF16)

    qi = lax.broadcasted_iota(jnp.int32, (group * w_, 2 * w_), 0) % w_
    kj = lax.broadcasted_iota(jnp.int32, (group * w_, 2 * w_), 1)
    band = (kj > qi) & (kj <= qi + w_)
    lo_half = lax.broadcasted_iota(jnp.int32, (w_, LANES), 1) < SWA_DH
    sinks4 = [jnp.concatenate([jnp.full((w_, 1), sinks_ref[kvh * group + gi], F32)
                               for gi in range(group)], axis=0) for kvh in range(SWA_KV_HEADS)]
    kv_heads = range(SWA_KV_HEADS)

    ones_v = jnp.ones((2 * w_, LANES), BF16)

    problems = [(blk, kvh) for blk in range(tm // w_) for kvh in kv_heads]

    def swa_scores(blk, kvh):
        pieces = []
        for pair in range(group // 2):
            cols = slice((kvh * (group // 2) + pair) * LANES, (kvh * (group // 2) + pair + 1) * LANES)
            pieces += [qlo_s[blk * w_:(blk + 1) * w_, cols], qhi_s[blk * w_:(blk + 1) * w_, cols]]
        q4 = jnp.concatenate(pieces, axis=0)
        kd = kvbuf[blk * w_:(blk + 2) * w_, kvh * LANES:(kvh + 1) * LANES]
        mask = band & jnp.logical_or(kj >= w_, j > 0) if blk == 0 else band
        return jnp.where(mask, _dot_nt(q4, kd), NEG_INF)

    def swa_outputs():
        scores = [swa_scores(blk, kvh) for blk, kvh in problems]
        maxes = [jnp.maximum(jnp.max(s, axis=-1, keepdims=True), sinks4[kvh])
                 for s, (blk, kvh) in zip(scores, problems)]
        probs = [jnp.exp(s - m).astype(BF16) for s, m in zip(scores, maxes)]
        for p, m, (blk, kvh) in zip(probs, maxes, problems):
            vd = kvbuf[blk * w_:(blk + 2) * w_, (2 + kvh) * LANES:(3 + kvh) * LANES]
            pv = _dot(p, jnp.concatenate([vd, ones_v], axis=1))
            o4 = pv[:, :LANES] / (pv[:, LANES:] + jnp.exp(sinks4[kvh] - m))
            for pair in range(group // 2):
                cols = slice((kvh * (group // 2) + pair) * LANES, (kvh * (group // 2) + pair + 1) * LANES)
                o_pair = jnp.where(lo_half, o4[(2 * pair) * w_:(2 * pair + 1) * w_],
                                   o4[(2 * pair + 1) * w_:(2 * pair + 2) * w_])
                osw_ref[0, blk * w_:(blk + 1) * w_, cols] = o_pair.astype(osw_ref.dtype)

    lax.fori_loop(0, n_chunks // WY_GROUP, wy_group_body, 0)
    lax.fori_loop(0, n_chunks, chunk_body, 0)
    deltanet_outputs()
    swa_outputs()
    kvbuf[0:w_, :] = kvbuf[tm:tm + w_, :]


def _mixers(h, w_mix, layer, convw, alog_row, dtb_row, normw_row, sinks):
    b, s, d = h.shape
    tm = min(TOKEN_TILE, s)
    n_qkv = 3 * DN_HEADS * DN_DK
    dn_w = DN_HEADS * DN_DV
    sw_w = SWA_HEADS * SWA_DH
    n_slots = (tm // DN_CHUNK) * DN_HEADS
    full = lambda shape: pl.BlockSpec(shape, lambda bi, ji: tuple(0 for _ in shape))
    tok = lambda width: pl.BlockSpec((1, tm, width), lambda bi, ji: (bi, ji, 0))
    return pl.pallas_call(
        functools.partial(_mixers_kernel, tm=tm),
        out_shape=(jax.ShapeDtypeStruct((b, s, dn_w), BF16), jax.ShapeDtypeStruct((b, s, sw_w), BF16)),
        grid=(b, s // tm),
        in_specs=[
            tok(d),
            pl.BlockSpec((None, d, w_mix.shape[-1]), lambda bi, ji: (layer, 0, 0)),
            full(convw.shape), full(alog_row.shape), full(dtb_row.shape), full(normw_row.shape),
            pl.BlockSpec(memory_space=pltpu.SMEM),
        ],
        out_specs=(tok(dn_w), tok(sw_w)),
        scratch_shapes=[
            pltpu.VMEM((tm + SUBLANES, n_qkv), F32),
            pltpu.VMEM((tm, dn_w), BF16),
            pltpu.VMEM((tm, dn_w), BF16),
            pltpu.VMEM((tm, dn_w), BF16),
            pltpu.VMEM((tm, dn_w), BF16),
            pltpu.VMEM((tm, dn_w), BF16),
            pltpu.VMEM((tm, 2 * dn_w), BF16),
            pltpu.VMEM((tm, dn_w), F32),
            pltpu.VMEM((tm, LANES), F32),
            pltpu.VMEM((tm, DN_HEADS * DN_CHUNK), F32),
            pltpu.VMEM((tm, LANES), F32),
            pltpu.VMEM((tm, dn_w), F32),
            pltpu.VMEM((n_slots, DN_DK + DN_CHUNK, DN_DK), BF16),
            pltpu.VMEM((n_slots, DN_DK, DN_DV), F32),
            pltpu.VMEM((DN_HEADS, DN_DK, DN_DV), F32),
            pltpu.VMEM((tm + WINDOW, 4 * LANES), BF16),
            pltpu.VMEM((tm, sw_w), BF16),
            pltpu.VMEM((tm, sw_w), BF16),
        ],
        compiler_params=pltpu.CompilerParams(
            dimension_semantics=("arbitrary", "arbitrary"), vmem_limit_bytes=VMEM_LIMIT_BYTES),
        name="mixers",
    )(h, w_mix, convw, alog_row, dtb_row, normw_row, sinks)


def _merge_kernel(h_ref, odn_ref, osw_ref, mem_ref, memg_ref, memb_ref, wmkv_ref, w_ref,
                  wb_ref, wo_ref, g_ref, b_ref, o_ref, kmem, vmem, merged, *, alpha):
    j = pl.program_id(1)
    xa_w = XA_HEADS * XA_DH

    @pl.when(j == 0)
    def _memory_kv():
        mem_n = _layer_norm(mem_ref[0], memg_ref[...], memb_ref[...])
        kvm = _dot(mem_n.astype(BF16), wmkv_ref[...])
        kmem[...] = kvm[:, :xa_w].astype(BF16)
        vmem[...] = kvm[:, xa_w:].astype(BF16)

    hf = h_ref[0]
    hb = hf.astype(BF16)
    xq = _dot(hb, w_ref[:, MRG_XAQ]) * (XA_DH ** -0.5)
    hsl = [slice(hd * XA_DH, (hd + 1) * XA_DH) for hd in range(XA_HEADS)]
    xqb = xq.astype(BF16)
    scores = [_dot_nt(xqb[:, sl], kmem[:, sl]) for sl in hsl]
    probs = [jnp.exp(s - jnp.max(s, axis=-1, keepdims=True)) for s in scores]
    denoms = [jnp.sum(p, axis=-1, keepdims=True) for p in probs]
    heads = [_dot(p.astype(BF16), vmem[:, sl]) / dn for p, dn, sl in zip(probs, denoms, hsl)]
    oxa = jnp.concatenate(heads, axis=1).astype(BF16)
    branches = (odn_ref[0], osw_ref[0], oxa)
    d = hf.shape[-1]
    for dc in range(d // MERGE_CHUNK):
        lo = dc * MERGE_CHUNK
        acc = None
        for n in range(N_BRANCH):
            g0 = MRG_GATE0 + n * d + lo
            gate = _sigmoid(_dot(hb, w_ref[:, g0:g0 + MERGE_CHUNK]))
            term = gate * _dot(branches[n], wb_ref[n, :, lo:lo + MERGE_CHUNK])
            acc = term if acc is None else acc + term
        merged[:, lo:lo + MERGE_CHUNK] = acc.astype(BF16)
    y = alpha * hf + _dot(merged[...], wo_ref[...])
    o_ref[0] = _layer_norm(y, g_ref[...], b_ref[...])


def _merge(h, odn, osw, mem, memg, memb, wmkv, w_mrg, wbranch, wout, layer, g, b_, alpha):
    b, s, d = h.shape
    tm = min(MERGE_TOKEN_TILE, s)
    n_mem = mem.shape[1]
    xa_w = XA_HEADS * XA_DH
    full = lambda shape: pl.BlockSpec(shape, lambda bi, ji: tuple(0 for _ in shape))
    tok = lambda width: pl.BlockSpec((1, tm, width), lambda bi, ji: (bi, ji, 0))
    layer_block = lambda w: pl.BlockSpec((None,) + w.shape[1:],
                                         lambda bi, ji: (layer,) + (0,) * (w.ndim - 1),
                                         pipeline_mode=pl.Buffered(1))
    return pl.pallas_call(
        functools.partial(_merge_kernel, alpha=alpha),
        out_shape=jax.ShapeDtypeStruct((b, s, d), F32),
        grid=(b, s // tm),
        in_specs=[
            tok(d), tok(odn.shape[-1]), tok(osw.shape[-1]),
            pl.BlockSpec((1, n_mem, d), lambda bi, ji: (bi, 0, 0)),
            full(memg.shape), full(memb.shape), layer_block(wmkv), layer_block(w_mrg),
            layer_block(wbranch), layer_block(wout), full(g.shape), full(b_.shape),
        ],
        out_specs=tok(d),
        scratch_shapes=[
            pltpu.VMEM((n_mem, xa_w), BF16),
            pltpu.VMEM((n_mem, xa_w), BF16),
            pltpu.VMEM((tm, d), BF16),
        ],
        compiler_params=pltpu.CompilerParams(
            dimension_semantics=("arbitrary", "arbitrary"), vmem_limit_bytes=VMEM_LIMIT_BYTES),
        name="merge",
    )(h, odn, osw, mem, memg, memb, wmkv, w_mrg, wbranch, wout, g, b_)


def _lane_row(vals, offset):
    return jnp.zeros((1, LANES), F32).at[0, offset:offset + vals.shape[0]].set(vals.astype(F32))


def _pack_kernel(main_ref, tail_ref, mix_ref, mrg_ref):
    o_b = MIX_QKV.stop
    shift = 2 * DN_HEADS
    n_mid = MIX_SWKV.stop - MIX_Z.start
    n_mrg = mrg_ref.shape[-1]
    mrg_lo = o_b + n_mid
    lane = lax.broadcasted_iota(jnp.int32, (main_ref.shape[0], LANES), 1)

    mix_ref[:, MIX_QKV] = main_ref[:, MIX_QKV].astype(BF16)
    mid_w = -(-(shift + n_mid) // LANES) * LANES
    mid = pltpu.roll(main_ref[:, o_b:o_b + mid_w], mid_w - shift, axis=1)
    mix_ref[:, MIX_Z.start:MIX_SWKV.stop] = mid[:, :n_mid].astype(BF16)
    first = main_ref[:, o_b:o_b + LANES]
    gate_lanes = (lane >= DN_HEADS) & (lane < 2 * DN_HEADS)
    mix_ref[:, MIX_BA.start:MIX_BA.start + LANES] = jnp.where(
        gate_lanes, pltpu.roll(first, DN_HEADS, axis=1), 0.0).astype(BF16)
    mix_ref[:, MIX_BA.start + LANES:MIX_BA.stop] = jnp.where(gate_lanes, first, 0.0).astype(BF16)

    big = pltpu.roll(main_ref[:, mrg_lo:mrg_lo + n_mrg], n_mrg - shift, axis=1)
    mrg_ref[:, :n_mrg - LANES] = big[:, :n_mrg - LANES].astype(BF16)
    tail = pltpu.roll(tail_ref[...], LANES - shift, axis=1)
    mrg_ref[:, n_mrg - LANES:] = jnp.where(lane < LANES - shift, big[:, n_mrg - LANES:], tail).astype(BF16)


def _pack_input_projection(w_in):
    depth, d, d_in = w_in.shape
    n_mrg = d_in - (MIX_QKV.stop + 2 * DN_HEADS + MIX_SWKV.stop - MIX_Z.start)
    main_w = (d_in // LANES) * LANES
    assert d_in - main_w == 2 * DN_HEADS and n_mrg % LANES == 0
    rows = PACK_ROWS
    return pl.pallas_call(
        _pack_kernel,
        out_shape=(jax.ShapeDtypeStruct((depth, d, MIX_WIDTH), BF16),
                   jax.ShapeDtypeStruct((depth, d, n_mrg), BF16)),
        grid=(depth, d // rows),
        in_specs=[pl.BlockSpec((None, rows, main_w), lambda l, i: (l, i, 0)),
                  pl.BlockSpec((None, rows, LANES), lambda l, i: (l, i, main_w // LANES))],
        out_specs=(pl.BlockSpec((None, rows, MIX_WIDTH), lambda l, i: (l, i, 0)),
                   pl.BlockSpec((None, rows, n_mrg), lambda l, i: (l, i, 0))),
        compiler_params=pltpu.CompilerParams(
            dimension_semantics=("arbitrary", "arbitrary"), vmem_limit_bytes=VMEM_LIMIT_BYTES),
        name="pack_w_in",
    )(w_in, w_in)


def _mixer_tables(conv_w, a_log, dt_bias, norm_w, sinks):
    return (conv_w.astype(F32), _lane_row(a_log, DN_HEADS), _lane_row(dt_bias, DN_HEADS),
            norm_w.reshape(1, -1).astype(F32), sinks.astype(F32))


def kernel(x, mem, mem_ln_g, mem_ln_b, ln_g, ln_b, ffn1_w_gu, ffn1_w_down, w_in, dn_conv_w, dn_a_log,
           dn_dt_bias, dn_norm_w, swa_sinks, w_mem_kv, w_branch, w_out, ffn2_w_gu, ffn2_w_down):
    b, s, d = x.shape
    depth = ln_g.shape[0]
    alpha = float((2 * depth) ** 0.25)
    row = lambda v: v.reshape(1, -1).astype(F32)

    w_mix, w_mrg = _pack_input_projection(w_in)
    ffn1_gu, ffn1_down = ffn1_w_gu.astype(BF16), (MACARON_SCALE * ffn1_w_down).astype(BF16)
    ffn2_gu, ffn2_down = ffn2_w_gu.astype(BF16), (MACARON_SCALE * ffn2_w_down).astype(BF16)
    w_mem_kv_b, w_branch_b, w_out_b = w_mem_kv.astype(BF16), w_branch.astype(BF16), w_out.astype(BF16)

    for l in range(depth):
        h = _ffn_ln(x.reshape(b * s, d), ffn1_gu, ffn1_down, l, row(ln_g[l, 0]), row(ln_b[l, 0]), alpha)
        h = h.reshape(b, s, d)
        odn, osw = _mixers(h, w_mix, l, *_mixer_tables(dn_conv_w[l], dn_a_log[l], dn_dt_bias[l],
                                                       dn_norm_w[l], swa_sinks[l]))
        h = _merge(h, odn, osw, mem, row(mem_ln_g), row(mem_ln_b), w_mem_kv_b, w_mrg, w_branch_b,
                   w_out_b, l, row(ln_g[l, 1]), row(ln_b[l, 1]), alpha)
        x = _ffn_ln(h.reshape(b * s, d), ffn2_gu, ffn2_down, l, row(ln_g[l, 2]), row(ln_b[l, 2]), alpha)
        x = x.reshape(b, s, d)
    return x
```

```python
import functools

import jax
import jax.numpy as jnp
from jax import lax
from jax.experimental import pallas as pl
from jax.experimental.pallas import tpu as pltpu

F32 = jnp.float32
BF16 = jnp.bfloat16

DN_HEADS, DN_DK, DN_DV, DN_CONV, DN_CHUNK = 4, 128, 128, 4, 64
SWA_HEADS, SWA_KV_HEADS, SWA_DH, WINDOW = 8, 2, 64, 128
XA_HEADS, XA_DH = 4, 128
N_BRANCH = 3
LN_EPS, RMS_EPS, NEG_INF = 1e-5, 1e-6, -1e30
MACARON_SCALE = 0.5

LANES = 128
SUBLANES = 8
VMEM_LIMIT_BYTES = 56 * 1024 * 1024

TOKEN_TILE = 512
MERGE_TOKEN_TILE = 1024
FFN_TOKEN_TILE = 1024
FF_CHUNK = 256
MERGE_CHUNK = 256
PACK_ROWS = 256
WY_GROUP = 8

_N_DN = DN_HEADS * DN_DK
MIX_QKV = slice(0, 3 * _N_DN)
MIX_Z = slice(MIX_QKV.stop, MIX_QKV.stop + DN_HEADS * DN_DV)
MIX_SWQ = slice(MIX_Z.stop, MIX_Z.stop + SWA_HEADS * SWA_DH)
MIX_SWKV = slice(MIX_SWQ.stop, MIX_SWQ.stop + 2 * SWA_KV_HEADS * SWA_DH)
MIX_BA = slice(MIX_SWKV.stop, MIX_SWKV.stop + 2 * LANES)
MIX_WIDTH = MIX_BA.stop
MRG_XAQ = slice(0, XA_HEADS * XA_DH)
MRG_GATE0 = MRG_XAQ.stop


def _dot(a, b):
    return jnp.dot(a, b, preferred_element_type=F32)


def _dot_nt(a, b):
    return lax.dot_general(a, b, (((1,), (1,)), ((), ())), preferred_element_type=F32)


def _dot_tn(a, b):
    return lax.dot_general(a, b, (((0,), (0,)), ((), ())), preferred_element_type=F32)


def _layer_norm(y, g, b):
    mu = jnp.mean(y, axis=-1, keepdims=True)
    d = y - mu
    var = jnp.mean(d * d, axis=-1, keepdims=True)
    return d * lax.rsqrt(var + LN_EPS) * g + b


def _sigmoid(x):
    return 0.5 * (jnp.tanh(0.5 * x) + 1.0)


def _silu(x):
    half = 0.5 * x
    return half * (jnp.tanh(half) + 1.0)


def _ffn_ln_kernel(x_ref, wgu_ref, wd_ref, g_ref, b_ref, o_ref, acc_ref, *, alpha):
    d_ff = wd_ref.shape[0]
    xf = x_ref[...]
    xb = xf.astype(BF16)
    for c in range(d_ff // FF_CHUNK):
        lo = c * FF_CHUNK
        gate = _dot(xb, wgu_ref[:, lo:lo + FF_CHUNK])
        up = _dot(xb, wgu_ref[:, d_ff + lo:d_ff + lo + FF_CHUNK])
        act = (_silu(gate) * up).astype(BF16)
        part = _dot(act, wd_ref[lo:lo + FF_CHUNK, :])
        if c == 0:
            acc_ref[...] = part
        else:
            acc_ref[...] += part
    y = alpha * xf + acc_ref[...]
    o_ref[...] = _layer_norm(y, g_ref[...], b_ref[...])


def _ffn_ln(x2d, w_gu, w_down, layer, g, b, alpha):
    t, d = x2d.shape
    _, d_ff, _ = w_down.shape
    tm = min(FFN_TOKEN_TILE, t)
    assert t % tm == 0 and d_ff % FF_CHUNK == 0 and d_ff % LANES == 0, (t, d_ff)
    return pl.pallas_call(
        functools.partial(_ffn_ln_kernel, alpha=alpha),
        out_shape=jax.ShapeDtypeStruct((t, d), F32),
        grid=(t // tm,),
        in_specs=[
            pl.BlockSpec((tm, d), lambda i: (i, 0)),
            pl.BlockSpec((None, d, 2 * d_ff), lambda i: (layer, 0, 0), pipeline_mode=pl.Buffered(1)),
            pl.BlockSpec((None, d_ff, d), lambda i: (layer, 0, 0), pipeline_mode=pl.Buffered(1)),
            pl.BlockSpec((1, d), lambda i: (0, 0)),
            pl.BlockSpec((1, d), lambda i: (0, 0)),
        ],
        out_specs=pl.BlockSpec((tm, d), lambda i: (i, 0)),
        scratch_shapes=[pltpu.VMEM((tm, d), F32)],
        compiler_params=pltpu.CompilerParams(
            dimension_semantics=("arbitrary",), vmem_limit_bytes=VMEM_LIMIT_BYTES),
        name="ffn_ln",
    )(x2d, w_gu, w_down, g, b)


def _split_bf16(x):
    hi = x.astype(BF16)
    lo = (x - hi.astype(F32)).astype(BF16)
    return hi, lo


def _split3_bf16(x):
    p1 = x.astype(BF16)
    r1 = x - p1.astype(F32)
    p2 = r1.astype(BF16)
    p3 = (r1 - p2.astype(F32)).astype(BF16)
    return p1, p2, p3


def _block_diag(y_l, n_blocks):
    c = y_l.shape[0]
    lane_block = lax.broadcasted_iota(jnp.int32, y_l.shape, 1) // c
    zero = jnp.zeros_like(y_l)
    return jnp.concatenate([jnp.where(lane_block == b, y_l, zero) for b in range(n_blocks)], axis=0)


def _blockwise_matmul(l_l, y_l, n_blocks):
    lh, ll = _split_bf16(l_l)
    yh, yl = _split_bf16(y_l)
    m = l_l.shape[0]
    top = _dot(jnp.concatenate([lh, ll], axis=0), _block_diag(yh, n_blocks))
    return top[:m] + top[m:] + _dot(lh, _block_diag(yl, n_blocks))


def _unit_lower_inverses(a_ls, n_blocks):
    c = a_ls[0].shape[0]
    row = lax.broadcasted_iota(jnp.int32, a_ls[0].shape, 0)
    col = lax.broadcasted_iota(jnp.int32, a_ls[0].shape, 1) % c
    eye = (row == col).astype(F32)
    ps = [eye - a_l for a_l in a_ls]
    ys = [_blockwise_matmul(-a_l, -a_l, n_blocks) for a_l in a_ls]
    power = 2
    while power < c:
        if 2 * power < c:
            stacked = [_blockwise_matmul(jnp.concatenate([p, y], axis=0), y, n_blocks)
                       for p, y in zip(ps, ys)]
            ps = [p + s[:c] for p, s in zip(ps, stacked)]
            ys = [s[c:] for s in stacked]
        else:
            ps = [p + _blockwise_matmul(p, y, n_blocks) for p, y in zip(ps, ys)]
        power *= 2
    return ps


def _mixers_kernel(h_ref, w_ref, convw_ref, alog_ref,
                   dtb_ref, normw_ref, sinks_ref, odn_ref, osw_ref,
                   xbuf, q_s, k_s, kb_s, qd_s, kt_s, rhs_s, z_s, gcum_s, gcol_s, elast_s, o0_s, pq_s, n_s, state, kvbuf, qlo_s, qhi_s,
                   *, tm):
    j = pl.program_id(1)
    n_qkv = 3 * DN_HEADS * DN_DK
    n_dn = DN_HEADS * DN_DK
    halo = SUBLANES
    c_ = DN_CHUNK
    n_chunks = tm // c_
    gate_lane = DN_HEADS

    @pl.when(j == 0)
    def _reset():
        xbuf[0:halo, :] = jnp.zeros((halo, n_qkv), F32)
        state[...] = jnp.zeros_like(state)
        kvbuf[0:WINDOW, :] = jnp.zeros((WINDOW, kvbuf.shape[1]), BF16)

    hb = h_ref[0].astype(BF16)

    pre = _dot(hb, w_ref[:, MIX_QKV])
    xbuf[halo:halo + tm, :] = pre
    cw = convw_ref[...]
    conv = cw[DN_CONV - 1:DN_CONV, :] * pre
    for s in range(1, DN_CONV):
        conv = conv + cw[DN_CONV - 1 - s:DN_CONV - s, :] * xbuf[halo - s:halo - s + tm, :]
    xbuf[0:halo, :] = xbuf[tm:tm + halo, :]
    qkv = _silu(conv)
    z_s[...] = _silu(_dot(hb, w_ref[:, MIX_Z]))

    ba = _dot(hb, w_ref[:, MIX_BA])
    beta = _sigmoid(ba[:, :LANES])
    g = -jnp.exp(alog_ref[...]) * jax.nn.softplus(ba[:, LANES:] + dtb_ref[...])
    row_c = lax.broadcasted_iota(jnp.int32, (c_, c_), 0)
    col_c = lax.broadcasted_iota(jnp.int32, (c_, c_), 1)
    tril_b = (row_c >= col_c).astype(BF16)
    g_parts = _split3_bf16(g)
    gcum_parts, etail_parts, elast_parts = [], [], []
    for ci in range(n_chunks):
        gc = sum(_dot(tril_b, part[ci * c_:(ci + 1) * c_, :]) for part in g_parts)
        glast = gc[c_ - 1:c_, :]
        gcum_parts.append(gc)
        etail_parts.append(jnp.exp(glast - gc))
        elast_parts.append(jnp.broadcast_to(jnp.exp(glast), (c_, LANES)))
    gcum = jnp.concatenate(gcum_parts, axis=0)
    gcum_s[...] = gcum
    elast_s[...] = jnp.concatenate(elast_parts, axis=0)
    sel_k = lax.broadcasted_iota(jnp.int32, (LANES, DN_HEADS * c_), 0)
    sel_n = lax.broadcasted_iota(jnp.int32, (LANES, DN_HEADS * c_), 1)
    sel_b = (sel_k == gate_lane + sel_n // c_).astype(BF16)
    gcol_s[...] = sum(_dot(part, sel_b) for part in _split3_bf16(gcum))
    e_g = jnp.exp(gcum)
    e_tail = jnp.concatenate(etail_parts, axis=0)
    for hd in range(DN_HEADS):
        lo = hd * DN_DK
        hs = slice(lo, lo + DN_DK)
        qh = qkv[:, lo:lo + DN_DK]
        qh = qh * (lax.rsqrt(jnp.sum(qh * qh, -1, keepdims=True) + RMS_EPS) * (DN_DK ** -0.5))
        kh = qkv[:, n_dn + lo:n_dn + lo + DN_DK]
        kh = kh * lax.rsqrt(jnp.sum(kh * kh, -1, keepdims=True) + RMS_EPS)
        vh = qkv[:, 2 * n_dn + lo:2 * n_dn + lo + DN_DV]
        gl = gate_lane + hd
        beta_b = jnp.broadcast_to(beta[:, gl:gl + 1], (tm, DN_DK))
        eg_b = jnp.broadcast_to(e_g[:, gl:gl + 1], (tm, DN_DK))
        et_b = jnp.broadcast_to(e_tail[:, gl:gl + 1], (tm, DN_DK))
        kb = kh * beta_b
        q_s[:, hs] = qh.astype(BF16)
        k_s[:, hs] = kh.astype(BF16)
        kb_s[:, hs] = kb.astype(BF16)
        qd_s[:, hs] = (qh * eg_b).astype(BF16)
        kt_s[:, hs] = (kh * et_b).astype(BF16)
        rhs_s[:, 2 * lo:2 * lo + DN_DV] = (vh * beta_b).astype(BF16)
        rhs_s[:, 2 * lo + DN_DV:2 * lo + DN_DV + DN_DK] = (kb * eg_b).astype(BF16)

    stack_w = DN_HEADS * c_
    row_l = lax.broadcasted_iota(jnp.int32, (c_, stack_w), 0)
    col_l = lax.broadcasted_iota(jnp.int32, (c_, stack_w), 1) % c_
    tril_l = row_l >= col_l
    strict_l = row_l > col_l
    lane = lax.broadcasted_iota(jnp.int32, (c_, LANES), 1)
    head_block = lax.broadcasted_iota(jnp.int32, (c_, n_dn), 1) // DN_DK
    ones_l = jnp.ones((c_, LANES), BF16)
    normw = normw_ref[...]
    zero_rhs = jnp.zeros((c_, DN_DV + DN_DK), BF16)
    heads = range(DN_HEADS)
    hsl = [slice(hd * DN_DK, (hd + 1) * DN_DK) for hd in heads]
    pairs = [slice((hd // 2) * LANES, (hd // 2 + 1) * LANES) for hd in heads]

    def wy_group_body(gi, carry):
        g0 = pl.multiple_of(gi * (WY_GROUP * c_), WY_GROUP * c_)
        chunk_rows = [pl.ds(g0 + ci * c_, c_) for ci in range(WY_GROUP)]
        decays = []
        for rows in chunk_rows:
            gc = gcum_s[rows, :]
            g_rows = jnp.concatenate(
                [jnp.where(lane == gate_lane + hd, gc, 0.0) for hd in range(DN_HEADS)], axis=0)
            grow_l = sum(_dot_nt(ones_l, part) for part in _split3_bf16(g_rows))
            decays.append(jnp.exp(jnp.where(tril_l, gcol_s[rows, :] - grow_l, NEG_INF)))
        kqs = []
        for rows in chunk_rows:
            k_rows = k_s[rows, :]
            k_bd = jnp.concatenate(
                [jnp.where(head_block == hd, k_rows, jnp.zeros_like(k_rows)) for hd in range(DN_HEADS)],
                axis=0)
            kqs.append(_dot_nt(jnp.concatenate([kb_s[rows, :], q_s[rows, :]], axis=0), k_bd))
        a_ls = [jnp.where(strict_l, kq[:c_] * dec, 0.0) for kq, dec in zip(kqs, decays)]
        qk_bs = [jnp.where(tril_l, kq[c_:] * dec, 0.0).astype(BF16) for kq, dec in zip(kqs, decays)]
        t_invs = _unit_lower_inverses(a_ls, DN_HEADS)
        sols = []
        for rows, t_inv in zip(chunk_rows, t_invs):
            t_inv_b = t_inv.astype(BF16)
            for hd in heads:
                rhs_h = rhs_s[rows, 2 * hd * DN_DK:2 * hd * DN_DK + DN_DV + DN_DK]
                rhs_pad = ([rhs_h, zero_rhs] if hd % 2 == 0 else [zero_rhs, rhs_h])
                sols.append(_dot(t_inv_b[:, pairs[hd]], jnp.concatenate(rhs_pad, axis=0)).astype(BF16))
        for n, (rows, qk_b) in enumerate(zip(chunk_rows, qk_bs)):
            for hd in heads:
                sol_b = sols[n * DN_HEADS + hd]
                sol_pad = ([sol_b, zero_rhs] if hd % 2 == 0 else [zero_rhs, sol_b])
                qk_uw = _dot(qk_b[:, pairs[hd]], jnp.concatenate(sol_pad, axis=0))
                kt_uw = _dot_tn(kt_s[rows, hsl[hd]], sol_b)
                q_eff = qd_s[rows, hsl[hd]].astype(F32) - qk_uw[:, DN_DV:]
                slot = (gi * WY_GROUP + n) * DN_HEADS + hd
                o0_s[rows, hsl[hd]] = qk_uw[:, :DN_DV]
                pq_s[slot] = jnp.concatenate([kt_uw[:, DN_DV:], q_eff], axis=0).astype(BF16)
                n_s[slot] = kt_uw[:, :DN_DV]
        return carry

    def chunk_body(ci, carry):
        r0 = pl.multiple_of(ci * c_, c_)
        rows = pl.ds(r0, c_)
        e_last = elast_s[pl.ds(r0, 1), :]
        s_old = [state[hd] for hd in heads]
        prods = [_dot(pq_s[ci * DN_HEADS + hd], s_old[hd].astype(BF16)) for hd in heads]
        for hd in heads:
            state[hd] = (s_old[hd] * e_last[:, gate_lane + hd:gate_lane + hd + 1]
                         - prods[hd][:DN_DK] + n_s[ci * DN_HEADS + hd])
        for hd in heads:
            o0_s[rows, hsl[hd]] += prods[hd][DN_DK:]
        return carry

    def deltanet_outputs():
        for hd in heads:
            o = o0_s[:, hsl[hd]]
            o = o * lax.rsqrt(jnp.mean(o * o, -1, keepdims=True) + RMS_EPS) * normw
            odn_ref[0, :, hsl[hd]] = (o * z_s[:, hsl[hd]]).astype(odn_ref.dtype)

    w_ = WINDOW
    group = SWA_HEADS // SWA_KV_HEADS
    swq = _dot(hb, w_ref[:, MIX_SWQ]) * (SWA_DH ** -0.5)
    kv = _dot(hb, w_ref[:, MIX_SWKV])
    lane_kv = lax.broadcasted_iota(jnp.int32, (tm, LANES), 1)
    first_half = lane_kv < SWA_DH
    for part in range(2):
        x = kv[:, part * LANES:(part + 1) * LANES]
        xr = pltpu.roll(x, SWA_DH, axis=1)
        dup0 = jnp.where(first_half, x, xr)
        dup1 = jnp.where(first_half, xr, x)
        kvbuf[w_:w_ + tm, (2 * part) * LANES:(2 * part + 1) * LANES] = dup0.astype(BF16)
        kvbuf[w_:w_ + tm, (2 * part + 1) * LANES:(2 * part + 2) * LANES] = dup1.astype(BF16)

    lane_q = lax.broadcasted_iota(jnp.int32, (tm, SWA_HEADS * SWA_DH), 1) % LANES
    qlo_s[...] = jnp.where(lane_q < SWA_DH, swq, 0.0).astype(BF16)
    qhi_s[...] = jnp.where(lane_q < SWA_DH, 0.0, swq).astype(BF16)

    qi = lax.broadcasted_iota(jnp.int32, (group * w_, 2 * w_), 0) % w_
    kj = lax.broadcasted_iota(jnp.int32, (group * w_, 2 * w_), 1)
    band = (kj > qi) & (kj <= qi + w_)
    lo_half = lax.broadcasted_iota(jnp.int32, (w_, LANES), 1) < SWA_DH
    sinks4 = [jnp.concatenate([jnp.full((w_, 1), sinks_ref[kvh * group + gi], F32)
                               for gi in range(group)], axis=0) for kvh in range(SWA_KV_HEADS)]
    kv_heads = range(SWA_KV_HEADS)

    ones_v = jnp.ones((2 * w_, LANES), BF16)

    problems = [(blk, kvh) for blk in range(tm // w_) for kvh in kv_heads]

    def swa_scores(blk, kvh):
        pieces = []
        for pair in range(group // 2):
            cols = slice((kvh * (group // 2) + pair) * LANES, (kvh * (group // 2) + pair + 1) * LANES)
            pieces += [qlo_s[blk * w_:(blk + 1) * w_, cols], qhi_s[blk * w_:(blk + 1) * w_, cols]]
        q4 = jnp.concatenate(pieces, axis=0)
        kd = kvbuf[blk * w_:(blk + 2) * w_, kvh * LANES:(kvh + 1) * LANES]
        mask = band & jnp.logical_or(kj >= w_, j > 0) if blk == 0 else band
        return jnp.where(mask, _dot_nt(q4, kd), NEG_INF)

    def swa_outputs():
        scores = [swa_scores(blk, kvh) for blk, kvh in problems]
        maxes = [jnp.maximum(jnp.max(s, axis=-1, keepdims=True), sinks4[kvh])
                 for s, (blk, kvh) in zip(scores, problems)]
        probs = [jnp.exp(s - m).astype(BF16) for s, m in zip(scores, maxes)]
        for p, m, (blk, kvh) in zip(probs, maxes, problems):
            vd = kvbuf[blk * w_:(blk + 2) * w_, (2 + kvh) * LANES:(3 + kvh) * LANES]
            pv = _dot(p, jnp.concatenate([vd, ones_v], axis=1))
            o4 = pv[:, :LANES] / (pv[:, LANES:] + jnp.exp(sinks4[kvh] - m))
            for pair in range(group // 2):
                cols = slice((kvh * (group // 2) + pair) * LANES, (kvh * (group // 2) + pair + 1) * LANES)
                o_pair = jnp.where(lo_half, o4[(2 * pair) * w_:(2 * pair + 1) * w_],
                                   o4[(2 * pair + 1) * w_:(2 * pair + 2) * w_])
                osw_ref[0, blk * w_:(blk + 1) * w_, cols] = o_pair.astype(osw_ref.dtype)

    lax.fori_loop(0, n_chunks // WY_GROUP, wy_group_body, 0)
    lax.fori_loop(0, n_chunks, chunk_body, 0)
    deltanet_outputs()
    swa_outputs()
    kvbuf[0:w_, :] = kvbuf[tm:tm + w_, :]


def _mixers(h, w_mix, layer, convw, alog_row, dtb_row, normw_row, sinks):
    b, s, d = h.shape
    tm = min(TOKEN_TILE, s)
    n_qkv = 3 * DN_HEADS * DN_DK
    dn_w = DN_HEADS * DN_DV
    sw_w = SWA_HEADS * SWA_DH
    n_slots = (tm // DN_CHUNK) * DN_HEADS
    assert s % tm == 0 and tm % WINDOW == 0 and tm % (WY_GROUP * DN_CHUNK) == 0, (s, tm)
    assert w_mix.shape[-1] == MIX_WIDTH and d % LANES == 0, (w_mix.shape, d)
    full = lambda shape: pl.BlockSpec(shape, lambda bi, ji: tuple(0 for _ in shape))
    tok = lambda width: pl.BlockSpec((1, tm, width), lambda bi, ji: (bi, ji, 0))
    return pl.pallas_call(
        functools.partial(_mixers_kernel, tm=tm),
        out_shape=(jax.ShapeDtypeStruct((b, s, dn_w), BF16), jax.ShapeDtypeStruct((b, s, sw_w), BF16)),
        grid=(b, s // tm),
        in_specs=[
            tok(d),
            pl.BlockSpec((None, d, w_mix.shape[-1]), lambda bi, ji: (layer, 0, 0)),
            full(convw.shape), full(alog_row.shape), full(dtb_row.shape), full(normw_row.shape),
            pl.BlockSpec(memory_space=pltpu.SMEM),
        ],
        out_specs=(tok(dn_w), tok(sw_w)),
        scratch_shapes=[
            pltpu.VMEM((tm + SUBLANES, n_qkv), F32),
            pltpu.VMEM((tm, dn_w), BF16),
            pltpu.VMEM((tm, dn_w), BF16),
            pltpu.VMEM((tm, dn_w), BF16),
            pltpu.VMEM((tm, dn_w), BF16),
            pltpu.VMEM((tm, dn_w), BF16),
            pltpu.VMEM((tm, 2 * dn_w), BF16),
            pltpu.VMEM((tm, dn_w), F32),
            pltpu.VMEM((tm, LANES), F32),
            pltpu.VMEM((tm, DN_HEADS * DN_CHUNK), F32),
            pltpu.VMEM((tm, LANES), F32),
            pltpu.VMEM((tm, dn_w), F32),
            pltpu.VMEM((n_slots, DN_DK + DN_CHUNK, DN_DK), BF16),
            pltpu.VMEM((n_slots, DN_DK, DN_DV), F32),
            pltpu.VMEM((DN_HEADS, DN_DK, DN_DV), F32),
            pltpu.VMEM((tm + WINDOW, 4 * LANES), BF16),
            pltpu.VMEM((tm, sw_w), BF16),
            pltpu.VMEM((tm, sw_w), BF16),
        ],
        compiler_params=pltpu.CompilerParams(
            dimension_semantics=("arbitrary", "arbitrary"), vmem_limit_bytes=VMEM_LIMIT_BYTES),
        name="mixers",
    )(h, w_mix, convw, alog_row, dtb_row, normw_row, sinks)


def _merge_kernel(h_ref, odn_ref, osw_ref, mem_ref, memg_ref, memb_ref, wmkv_ref, w_ref,
                  wb_ref, wo_ref, g_ref, b_ref, o_ref, kmem, vmem, merged, *, alpha):
    j = pl.program_id(1)
    xa_w = XA_HEADS * XA_DH

    @pl.when(j == 0)
    def _memory_kv():
        mem_n = _layer_norm(mem_ref[0], memg_ref[...], memb_ref[...])
        kvm = _dot(mem_n.astype(BF16), wmkv_ref[...])
        kmem[...] = kvm[:, :xa_w].astype(BF16)
        vmem[...] = kvm[:, xa_w:].astype(BF16)

    hf = h_ref[0]
    hb = hf.astype(BF16)
    xq = _dot(hb, w_ref[:, MRG_XAQ]) * (XA_DH ** -0.5)
    hsl = [slice(hd * XA_DH, (hd + 1) * XA_DH) for hd in range(XA_HEADS)]
    xqb = xq.astype(BF16)
    scores = [_dot_nt(xqb[:, sl], kmem[:, sl]) for sl in hsl]
    probs = [jnp.exp(s - jnp.max(s, axis=-1, keepdims=True)) for s in scores]
    denoms = [jnp.sum(p, axis=-1, keepdims=True) for p in probs]
    heads = [_dot(p.astype(BF16), vmem[:, sl]) / dn for p, dn, sl in zip(probs, denoms, hsl)]
    oxa = jnp.concatenate(heads, axis=1).astype(BF16)
    branches = (odn_ref[0], osw_ref[0], oxa)
    d = hf.shape[-1]
    for dc in range(d // MERGE_CHUNK):
        lo = dc * MERGE_CHUNK
        acc = None
        for n in range(N_BRANCH):
            g0 = MRG_GATE0 + n * d + lo
            gate = _sigmoid(_dot(hb, w_ref[:, g0:g0 + MERGE_CHUNK]))
            term = gate * _dot(branches[n], wb_ref[n, :, lo:lo + MERGE_CHUNK])
            acc = term if acc is None else acc + term
        merged[:, lo:lo + MERGE_CHUNK] = acc.astype(BF16)
    y = alpha * hf + _dot(merged[...], wo_ref[...])
    o_ref[0] = _layer_norm(y, g_ref[...], b_ref[...])


def _merge(h, odn, osw, mem, memg, memb, wmkv, w_mrg, wbranch, wout, layer, g, b_, alpha):
    b, s, d = h.shape
    tm = min(MERGE_TOKEN_TILE, s)
    n_mem = mem.shape[1]
    xa_w = XA_HEADS * XA_DH
    assert s % tm == 0 and d % MERGE_CHUNK == 0 and w_mrg.shape[-1] == MRG_GATE0 + N_BRANCH * d, (s, d)
    full = lambda shape: pl.BlockSpec(shape, lambda bi, ji: tuple(0 for _ in shape))
    tok = lambda width: pl.BlockSpec((1, tm, width), lambda bi, ji: (bi, ji, 0))
    layer_block = lambda w: pl.BlockSpec((None,) + w.shape[1:],
                                         lambda bi, ji: (layer,) + (0,) * (w.ndim - 1),
                                         pipeline_mode=pl.Buffered(1))
    return pl.pallas_call(
        functools.partial(_merge_kernel, alpha=alpha),
        out_shape=jax.ShapeDtypeStruct((b, s, d), F32),
        grid=(b, s // tm),
        in_specs=[
            tok(d), tok(odn.shape[-1]), tok(osw.shape[-1]),
            pl.BlockSpec((1, n_mem, d), lambda bi, ji: (bi, 0, 0)),
            full(memg.shape), full(memb.shape), layer_block(wmkv), layer_block(w_mrg),
            layer_block(wbranch), layer_block(wout), full(g.shape), full(b_.shape),
        ],
        out_specs=tok(d),
        scratch_shapes=[
            pltpu.VMEM((n_mem, xa_w), BF16),
            pltpu.VMEM((n_mem, xa_w), BF16),
            pltpu.VMEM((tm, d), BF16),
        ],
        compiler_params=pltpu.CompilerParams(
            dimension_semantics=("arbitrary", "arbitrary"), vmem_limit_bytes=VMEM_LIMIT_BYTES),
        name="merge",
    )(h, odn, osw, mem, memg, memb, wmkv, w_mrg, wbranch, wout, g, b_)


def _lane_row(vals, offset):
    return jnp.zeros((1, LANES), F32).at[0, offset:offset + vals.shape[0]].set(vals.astype(F32))


def _pack_kernel(main_ref, tail_ref, mix_ref, mrg_ref):
    o_b = MIX_QKV.stop
    shift = 2 * DN_HEADS
    n_mid = MIX_SWKV.stop - MIX_Z.start
    n_mrg = mrg_ref.shape[-1]
    mrg_lo = o_b + n_mid
    lane = lax.broadcasted_iota(jnp.int32, (main_ref.shape[0], LANES), 1)

    mix_ref[:, MIX_QKV] = main_ref[:, MIX_QKV].astype(BF16)
    mid_w = -(-(shift + n_mid) // LANES) * LANES
    mid = pltpu.roll(main_ref[:, o_b:o_b + mid_w], mid_w - shift, axis=1)
    mix_ref[:, MIX_Z.start:MIX_SWKV.stop] = mid[:, :n_mid].astype(BF16)
    first = main_ref[:, o_b:o_b + LANES]
    gate_lanes = (lane >= DN_HEADS) & (lane < 2 * DN_HEADS)
    mix_ref[:, MIX_BA.start:MIX_BA.start + LANES] = jnp.where(
        gate_lanes, pltpu.roll(first, DN_HEADS, axis=1), 0.0).astype(BF16)
    mix_ref[:, MIX_BA.start + LANES:MIX_BA.stop] = jnp.where(gate_lanes, first, 0.0).astype(BF16)

    big = pltpu.roll(main_ref[:, mrg_lo:mrg_lo + n_mrg], n_mrg - shift, axis=1)
    mrg_ref[:, :n_mrg - LANES] = big[:, :n_mrg - LANES].astype(BF16)
    tail = pltpu.roll(tail_ref[...], LANES - shift, axis=1)
    mrg_ref[:, n_mrg - LANES:] = jnp.where(lane < LANES - shift, big[:, n_mrg - LANES:], tail).astype(BF16)


def _pack_input_projection(w_in):
    depth, d, d_in = w_in.shape
    n_mrg = d_in - (MIX_QKV.stop + 2 * DN_HEADS + MIX_SWKV.stop - MIX_Z.start)
    main_w = (d_in // LANES) * LANES
    assert d_in - main_w == 2 * DN_HEADS and n_mrg % LANES == 0
    rows = PACK_ROWS
    return pl.pallas_call(
        _pack_kernel,
        out_shape=(jax.ShapeDtypeStruct((depth, d, MIX_WIDTH), BF16),
                   jax.ShapeDtypeStruct((depth, d, n_mrg), BF16)),
        grid=(depth, d // rows),
        in_specs=[pl.BlockSpec((None, rows, main_w), lambda l, i: (l, i, 0)),
                  pl.BlockSpec((None, rows, LANES), lambda l, i: (l, i, main_w // LANES))],
        out_specs=(pl.BlockSpec((None, rows, MIX_WIDTH), lambda l, i: (l, i, 0)),
                   pl.BlockSpec((None, rows, n_mrg), lambda l, i: (l, i, 0))),
        compiler_params=pltpu.CompilerParams(
            dimension_semantics=("arbitrary", "arbitrary"), vmem_limit_bytes=VMEM_LIMIT_BYTES),
        name="pack_w_in",
    )(w_in, w_in)


def _mixer_tables(conv_w, a_log, dt_bias, norm_w, sinks):
    return (conv_w.astype(F32), _lane_row(a_log, DN_HEADS), _lane_row(dt_bias, DN_HEADS),
            norm_w.reshape(1, -1).astype(F32), sinks.astype(F32))


def kernel(x, mem, mem_ln_g, mem_ln_b, ln_g, ln_b, ffn1_w_gu, ffn1_w_down, w_in, dn_conv_w, dn_a_log,
           dn_dt_bias, dn_norm_w, swa_sinks, w_mem_kv, w_branch, w_out, ffn2_w_gu, ffn2_w_down):
    b, s, d = x.shape
    depth = ln_g.shape[0]
    alpha = float((2 * depth) ** 0.25)
    row = lambda v: v.reshape(1, -1).astype(F32)

    w_mix, w_mrg = _pack_input_projection(w_in)
    ffn1_gu, ffn1_down = ffn1_w_gu.astype(BF16), (MACARON_SCALE * ffn1_w_down).astype(BF16)
    ffn2_gu, ffn2_down = ffn2_w_gu.astype(BF16), (MACARON_SCALE * ffn2_w_down).astype(BF16)
    w_mem_kv_b, w_branch_b, w_out_b = w_mem_kv.astype(BF16), w_branch.astype(BF16), w_out.astype(BF16)

    for l in range(depth):
        h = _ffn_ln(x.reshape(b * s, d), ffn1_gu, ffn1_down, l, row(ln_g[l, 0]), row(ln_b[l, 0]), alpha)
        h = h.reshape(b, s, d)
        odn, osw = _mixers(h, w_mix, l, *_mixer_tables(dn_conv_w[l], dn_a_log[l], dn_dt_bias[l],
                                                       dn_norm_w[l], swa_sinks[l]))
        h = _merge(h, odn, osw, mem, row(mem_ln_g), row(mem_ln_b), w_mem_kv_b, w_mrg, w_branch_b,
                   w_out_b, l, row(ln_g[l, 1]), row(ln_b[l, 1]), alpha)
        x = _ffn_ln(h.reshape(b * s, d), ffn2_gu, ffn2_down, l, row(ln_g[l, 2]), row(ln_b[l, 2]), alpha)
        x = x.reshape(b, s, d)
    return x
```

```python
import functools

import jax
import jax.numpy as jnp
from jax import lax
from jax.experimental import pallas as pl
from jax.experimental.pallas import tpu as pltpu

F32 = jnp.float32
BF16 = jnp.bfloat16

DN_HEADS, DN_DK, DN_DV, DN_CONV, DN_CHUNK = 4, 128, 128, 4, 64
SWA_HEADS, SWA_KV_HEADS, SWA_DH, WINDOW = 8, 2, 64, 128
XA_HEADS, XA_DH = 4, 128
N_BRANCH = 3
LN_EPS, RMS_EPS, NEG_INF = 1e-5, 1e-6, -1e30
MACARON_SCALE = 0.5

LANES = 128
SUBLANES = 8
VMEM_LIMIT_BYTES = 56 * 1024 * 1024

TOKEN_TILE = 512
MERGE_TOKEN_TILE = 512
FFN_TOKEN_TILE = 1024
FF_CHUNK = 256
MERGE_CHUNK = 256
PACK_ROWS = 256
WY_GROUP = 8

_N_DN = DN_HEADS * DN_DK
MIX_QKV = slice(0, 3 * _N_DN)
MIX_Z = slice(MIX_QKV.stop, MIX_QKV.stop + DN_HEADS * DN_DV)
MIX_SWQ = slice(MIX_Z.stop, MIX_Z.stop + SWA_HEADS * SWA_DH)
MIX_SWKV = slice(MIX_SWQ.stop, MIX_SWQ.stop + 2 * SWA_KV_HEADS * SWA_DH)
MIX_BA = slice(MIX_SWKV.stop, MIX_SWKV.stop + 2 * LANES)
MIX_WIDTH = MIX_BA.stop
MRG_XAQ = slice(0, XA_HEADS * XA_DH)
MRG_GATE0 = MRG_XAQ.stop


def _dot(a, b):
    return jnp.dot(a, b, preferred_element_type=F32)


def _dot_nt(a, b):
    return lax.dot_general(a, b, (((1,), (1,)), ((), ())), preferred_element_type=F32)


def _dot_tn(a, b):
    return lax.dot_general(a, b, (((0,), (0,)), ((), ())), preferred_element_type=F32)


def _layer_norm(y, g, b):
    mu = jnp.mean(y, axis=-1, keepdims=True)
    d = y - mu
    var = jnp.mean(d * d, axis=-1, keepdims=True)
    return d * lax.rsqrt(var + LN_EPS) * g + b


def _sigmoid(x):
    return 0.5 * (jnp.tanh(0.5 * x) + 1.0)


def _silu(x):
    half = 0.5 * x
    return half * (jnp.tanh(half) + 1.0)


def _ffn_ln_kernel(x_ref, wgu_ref, wd_ref, g_ref, b_ref, o_ref, acc_ref, *, alpha):
    d_ff = wd_ref.shape[0]
    xf = x_ref[...]
    xb = xf.astype(BF16)
    for c in range(d_ff // FF_CHUNK):
        lo = c * FF_CHUNK
        gate = _dot(xb, wgu_ref[:, lo:lo + FF_CHUNK])
        up = _dot(xb, wgu_ref[:, d_ff + lo:d_ff + lo + FF_CHUNK])
        act = (_silu(gate) * up).astype(BF16)
        part = _dot(act, wd_ref[lo:lo + FF_CHUNK, :])
        if c == 0:
            acc_ref[...] = part
        else:
            acc_ref[...] += part
    y = alpha * xf + acc_ref[...]
    o_ref[...] = _layer_norm(y, g_ref[...], b_ref[...])


def _ffn_ln(x2d, w_gu, w_down, layer, g, b, alpha):
    t, d = x2d.shape
    _, d_ff, _ = w_down.shape
    tm = min(FFN_TOKEN_TILE, t)
    assert t % tm == 0 and d_ff % FF_CHUNK == 0 and d_ff % LANES == 0, (t, d_ff)
    return pl.pallas_call(
        functools.partial(_ffn_ln_kernel, alpha=alpha),
        out_shape=jax.ShapeDtypeStruct((t, d), F32),
        grid=(t // tm,),
        in_specs=[
            pl.BlockSpec((tm, d), lambda i: (i, 0)),
            pl.BlockSpec((None, d, 2 * d_ff), lambda i: (layer, 0, 0), pipeline_mode=pl.Buffered(1)),
            pl.BlockSpec((None, d_ff, d), lambda i: (layer, 0, 0), pipeline_mode=pl.Buffered(1)),
            pl.BlockSpec((1, d), lambda i: (0, 0)),
            pl.BlockSpec((1, d), lambda i: (0, 0)),
        ],
        out_specs=pl.BlockSpec((tm, d), lambda i: (i, 0)),
        scratch_shapes=[pltpu.VMEM((tm, d), F32)],
        compiler_params=pltpu.CompilerParams(
            dimension_semantics=("arbitrary",), vmem_limit_bytes=VMEM_LIMIT_BYTES),
        name="ffn_ln",
    )(x2d, w_gu, w_down, g, b)


def _split_bf16(x):
    hi = x.astype(BF16)
    lo = (x - hi.astype(F32)).astype(BF16)
    return hi, lo


def _split3_bf16(x):
    p1 = x.astype(BF16)
    r1 = x - p1.astype(F32)
    p2 = r1.astype(BF16)
    p3 = (r1 - p2.astype(F32)).astype(BF16)
    return p1, p2, p3


def _block_diag(y_l, n_blocks):
    c = y_l.shape[0]
    lane_block = lax.broadcasted_iota(jnp.int32, y_l.shape, 1) // c
    zero = jnp.zeros_like(y_l)
    return jnp.concatenate([jnp.where(lane_block == b, y_l, zero) for b in range(n_blocks)], axis=0)


def _blockwise_matmul(l_l, y_l, n_blocks):
    lh, ll = _split_bf16(l_l)
    yh, yl = _split_bf16(y_l)
    m = l_l.shape[0]
    top = _dot(jnp.concatenate([lh, ll], axis=0), _block_diag(yh, n_blocks))
    return top[:m] + top[m:] + _dot(lh, _block_diag(yl, n_blocks))


def _unit_lower_inverses(a_ls, n_blocks):
    c = a_ls[0].shape[0]
    row = lax.broadcasted_iota(jnp.int32, a_ls[0].shape, 0)
    col = lax.broadcasted_iota(jnp.int32, a_ls[0].shape, 1) % c
    eye = (row == col).astype(F32)
    ps = [eye - a_l for a_l in a_ls]
    ys = [_blockwise_matmul(-a_l, -a_l, n_blocks) for a_l in a_ls]
    power = 2
    while power < c:
        if 2 * power < c:
            stacked = [_blockwise_matmul(jnp.concatenate([p, y], axis=0), y, n_blocks)
                       for p, y in zip(ps, ys)]
            ps = [p + s[:c] for p, s in zip(ps, stacked)]
            ys = [s[c:] for s in stacked]
        else:
            ps = [p + _blockwise_matmul(p, y, n_blocks) for p, y in zip(ps, ys)]
        power *= 2
    return ps


def _mixers_kernel(h_ref, w_ref, convw_ref, alog_ref,
                   dtb_ref, normw_ref, sinks_ref, odn_ref, osw_ref,
                   xbuf, q_s, k_s, kb_s, qd_s, kt_s, rhs_s, z_s, gcum_s, gcol_s, elast_s, o0_s, pq_s, n_s, state, kvbuf, qlo_s, qhi_s,
                   *, tm):
    j = pl.program_id(1)
    n_qkv = 3 * DN_HEADS * DN_DK
    n_dn = DN_HEADS * DN_DK
    halo = SUBLANES
    c_ = DN_CHUNK
    n_chunks = tm // c_
    gate_lane = DN_HEADS

    @pl.when(j == 0)
    def _reset():
        xbuf[0:halo, :] = jnp.zeros((halo, n_qkv), F32)
        state[...] = jnp.zeros_like(state)
        kvbuf[0:WINDOW, :] = jnp.zeros((WINDOW, kvbuf.shape[1]), BF16)

    hb = h_ref[0].astype(BF16)

    pre = _dot(hb, w_ref[:, MIX_QKV])
    xbuf[halo:halo + tm, :] = pre
    cw = convw_ref[...]
    conv = cw[DN_CONV - 1:DN_CONV, :] * pre
    for s in range(1, DN_CONV):
        conv = conv + cw[DN_CONV - 1 - s:DN_CONV - s, :] * xbuf[halo - s:halo - s + tm, :]
    xbuf[0:halo, :] = xbuf[tm:tm + halo, :]
    qkv = _silu(conv)
    z_s[...] = _silu(_dot(hb, w_ref[:, MIX_Z]))

    ba = _dot(hb, w_ref[:, MIX_BA])
    beta = _sigmoid(ba[:, :LANES])
    g = -jnp.exp(alog_ref[...]) * jax.nn.softplus(ba[:, LANES:] + dtb_ref[...])
    row_c = lax.broadcasted_iota(jnp.int32, (c_, c_), 0)
    col_c = lax.broadcasted_iota(jnp.int32, (c_, c_), 1)
    tril_b = (row_c >= col_c).astype(BF16)
    g_parts = _split3_bf16(g)
    gcum_parts, etail_parts, elast_parts = [], [], []
    for ci in range(n_chunks):
        gc = sum(_dot(tril_b, part[ci * c_:(ci + 1) * c_, :]) for part in g_parts)
        glast = gc[c_ - 1:c_, :]
        gcum_parts.append(gc)
        etail_parts.append(jnp.exp(glast - gc))
        elast_parts.append(jnp.broadcast_to(jnp.exp(glast), (c_, LANES)))
    gcum = jnp.concatenate(gcum_parts, axis=0)
    gcum_s[...] = gcum
    elast_s[...] = jnp.concatenate(elast_parts, axis=0)
    sel_k = lax.broadcasted_iota(jnp.int32, (LANES, DN_HEADS * c_), 0)
    sel_n = lax.broadcasted_iota(jnp.int32, (LANES, DN_HEADS * c_), 1)
    sel_b = (sel_k == gate_lane + sel_n // c_).astype(BF16)
    gcol_s[...] = sum(_dot(part, sel_b) for part in _split3_bf16(gcum))
    e_g = jnp.exp(gcum)
    e_tail = jnp.concatenate(etail_parts, axis=0)
    for hd in range(DN_HEADS):
        lo = hd * DN_DK
        hs = slice(lo, lo + DN_DK)
        qh = qkv[:, lo:lo + DN_DK]
        qh = qh * (lax.rsqrt(jnp.sum(qh * qh, -1, keepdims=True) + RMS_EPS) * (DN_DK ** -0.5))
        kh = qkv[:, n_dn + lo:n_dn + lo + DN_DK]
        kh = kh * lax.rsqrt(jnp.sum(kh * kh, -1, keepdims=True) + RMS_EPS)
        vh = qkv[:, 2 * n_dn + lo:2 * n_dn + lo + DN_DV]
        gl = gate_lane + hd
        beta_b = jnp.broadcast_to(beta[:, gl:gl + 1], (tm, DN_DK))
        eg_b = jnp.broadcast_to(e_g[:, gl:gl + 1], (tm, DN_DK))
        et_b = jnp.broadcast_to(e_tail[:, gl:gl + 1], (tm, DN_DK))
        kb = kh * beta_b
        q_s[:, hs] = qh.astype(BF16)
        k_s[:, hs] = kh.astype(BF16)
        kb_s[:, hs] = kb.astype(BF16)
        qd_s[:, hs] = (qh * eg_b).astype(BF16)
        kt_s[:, hs] = (kh * et_b).astype(BF16)
        rhs_s[:, 2 * lo:2 * lo + DN_DV] = (vh * beta_b).astype(BF16)
        rhs_s[:, 2 * lo + DN_DV:2 * lo + DN_DV + DN_DK] = (kb * eg_b).astype(BF16)

    stack_w = DN_HEADS * c_
    row_l = lax.broadcasted_iota(jnp.int32, (c_, stack_w), 0)
    col_l = lax.broadcasted_iota(jnp.int32, (c_, stack_w), 1) % c_
    tril_l = row_l >= col_l
    strict_l = row_l > col_l
    lane = lax.broadcasted_iota(jnp.int32, (c_, LANES), 1)
    head_block = lax.broadcasted_iota(jnp.int32, (c_, n_dn), 1) // DN_DK
    ones_l = jnp.ones((c_, LANES), BF16)
    normw = normw_ref[...]
    zero_rhs = jnp.zeros((c_, DN_DV + DN_DK), BF16)
    heads = range(DN_HEADS)
    hsl = [slice(hd * DN_DK, (hd + 1) * DN_DK) for hd in heads]
    pairs = [slice((hd // 2) * LANES, (hd // 2 + 1) * LANES) for hd in heads]

    def wy_group_body(gi, carry):
        g0 = pl.multiple_of(gi * (WY_GROUP * c_), WY_GROUP * c_)
        chunk_rows = [pl.ds(g0 + ci * c_, c_) for ci in range(WY_GROUP)]
        decays = []
        for rows in chunk_rows:
            gc = gcum_s[rows, :]
            g_rows = jnp.concatenate(
                [jnp.where(lane == gate_lane + hd, gc, 0.0) for hd in range(DN_HEADS)], axis=0)
            grow_l = sum(_dot_nt(ones_l, part) for part in _split3_bf16(g_rows))
            decays.append(jnp.exp(jnp.where(tril_l, gcol_s[rows, :] - grow_l, NEG_INF)))
        kqs = []
        for rows in chunk_rows:
            k_rows = k_s[rows, :]
            k_bd = jnp.concatenate(
                [jnp.where(head_block == hd, k_rows, jnp.zeros_like(k_rows)) for hd in range(DN_HEADS)],
                axis=0)
            kqs.append(_dot_nt(jnp.concatenate([kb_s[rows, :], q_s[rows, :]], axis=0), k_bd))
        a_ls = [jnp.where(strict_l, kq[:c_] * dec, 0.0) for kq, dec in zip(kqs, decays)]
        qk_bs = [jnp.where(tril_l, kq[c_:] * dec, 0.0).astype(BF16) for kq, dec in zip(kqs, decays)]
        t_invs = _unit_lower_inverses(a_ls, DN_HEADS)
        sols = []
        for rows, t_inv in zip(chunk_rows, t_invs):
            t_inv_b = t_inv.astype(BF16)
            for hd in heads:
                rhs_h = rhs_s[rows, 2 * hd * DN_DK:2 * hd * DN_DK + DN_DV + DN_DK]
                rhs_pad = ([rhs_h, zero_rhs] if hd % 2 == 0 else [zero_rhs, rhs_h])
                sols.append(_dot(t_inv_b[:, pairs[hd]], jnp.concatenate(rhs_pad, axis=0)).astype(BF16))
        for n, (rows, qk_b) in enumerate(zip(chunk_rows, qk_bs)):
            for hd in heads:
                sol_b = sols[n * DN_HEADS + hd]
                sol_pad = ([sol_b, zero_rhs] if hd % 2 == 0 else [zero_rhs, sol_b])
                qk_uw = _dot(qk_b[:, pairs[hd]], jnp.concatenate(sol_pad, axis=0))
                kt_uw = _dot_tn(kt_s[rows, hsl[hd]], sol_b)
                q_eff = qd_s[rows, hsl[hd]].astype(F32) - qk_uw[:, DN_DV:]
                slot = (gi * WY_GROUP + n) * DN_HEADS + hd
                o0_s[rows, hsl[hd]] = qk_uw[:, :DN_DV]
                pq_s[slot] = jnp.concatenate([kt_uw[:, DN_DV:], q_eff], axis=0).astype(BF16)
                n_s[slot] = kt_uw[:, :DN_DV]
        return carry

    def chunk_body(ci, carry):
        r0 = pl.multiple_of(ci * c_, c_)
        rows = pl.ds(r0, c_)
        e_last = elast_s[pl.ds(r0, 1), :]
        s_old = [state[hd] for hd in heads]
        prods = [_dot(pq_s[ci * DN_HEADS + hd], s_old[hd].astype(BF16)) for hd in heads]
        for hd in heads:
            state[hd] = (s_old[hd] * e_last[:, gate_lane + hd:gate_lane + hd + 1]
                         - prods[hd][:DN_DK] + n_s[ci * DN_HEADS + hd])
        for hd in heads:
            o0_s[rows, hsl[hd]] += prods[hd][DN_DK:]
        return carry

    def deltanet_outputs():
        for hd in heads:
            o = o0_s[:, hsl[hd]]
            o = o * lax.rsqrt(jnp.mean(o * o, -1, keepdims=True) + RMS_EPS) * normw
            odn_ref[0, :, hsl[hd]] = (o * z_s[:, hsl[hd]]).astype(odn_ref.dtype)

    w_ = WINDOW
    group = SWA_HEADS // SWA_KV_HEADS
    swq = _dot(hb, w_ref[:, MIX_SWQ]) * (SWA_DH ** -0.5)
    kv = _dot(hb, w_ref[:, MIX_SWKV])
    lane_kv = lax.broadcasted_iota(jnp.int32, (tm, LANES), 1)
    first_half = lane_kv < SWA_DH
    for part in range(2):
        x = kv[:, part * LANES:(part + 1) * LANES]
        xr = pltpu.roll(x, SWA_DH, axis=1)
        dup0 = jnp.where(first_half, x, xr)
        dup1 = jnp.where(first_half, xr, x)
        kvbuf[w_:w_ + tm, (2 * part) * LANES:(2 * part + 1) * LANES] = dup0.astype(BF16)
        kvbuf[w_:w_ + tm, (2 * part + 1) * LANES:(2 * part + 2) * LANES] = dup1.astype(BF16)

    lane_q = lax.broadcasted_iota(jnp.int32, (tm, SWA_HEADS * SWA_DH), 1) % LANES
    qlo_s[...] = jnp.where(lane_q < SWA_DH, swq, 0.0).astype(BF16)
    qhi_s[...] = jnp.where(lane_q < SWA_DH, 0.0, swq).astype(BF16)

    qi = lax.broadcasted_iota(jnp.int32, (group * w_, 2 * w_), 0) % w_
    kj = lax.broadcasted_iota(jnp.int32, (group * w_, 2 * w_), 1)
    band = (kj > qi) & (kj <= qi + w_)
    lo_half = lax.broadcasted_iota(jnp.int32, (w_, LANES), 1) < SWA_DH
    sinks4 = [jnp.concatenate([jnp.full((w_, 1), sinks_ref[kvh * group + gi], F32)
                               for gi in range(group)], axis=0) for kvh in range(SWA_KV_HEADS)]
    kv_heads = range(SWA_KV_HEADS)

    ones_v = jnp.ones((2 * w_, LANES), BF16)

    problems = [(blk, kvh) for blk in range(tm // w_) for kvh in kv_heads]

    def swa_scores(blk, kvh):
        pieces = []
        for pair in range(group // 2):
            cols = slice((kvh * (group // 2) + pair) * LANES, (kvh * (group // 2) + pair + 1) * LANES)
            pieces += [qlo_s[blk * w_:(blk + 1) * w_, cols], qhi_s[blk * w_:(blk + 1) * w_, cols]]
        q4 = jnp.concatenate(pieces, axis=0)
        kd = kvbuf[blk * w_:(blk + 2) * w_, kvh * LANES:(kvh + 1) * LANES]
        mask = band & jnp.logical_or(kj >= w_, j > 0) if blk == 0 else band
        return jnp.where(mask, _dot_nt(q4, kd), NEG_INF)

    def swa_outputs():
        scores = [swa_scores(blk, kvh) for blk, kvh in problems]
        maxes = [jnp.maximum(jnp.max(s, axis=-1, keepdims=True), sinks4[kvh])
                 for s, (blk, kvh) in zip(scores, problems)]
        probs = [jnp.exp(s - m).astype(BF16) for s, m in zip(scores, maxes)]
        for p, m, (blk, kvh) in zip(probs, maxes, problems):
            vd = kvbuf[blk * w_:(blk + 2) * w_, (2 + kvh) * LANES:(3 + kvh) * LANES]
            pv = _dot(p, jnp.concatenate([vd, ones_v], axis=1))
            o4 = pv[:, :LANES] / (pv[:, LANES:] + jnp.exp(sinks4[kvh] - m))
            for pair in range(group // 2):
                cols = slice((kvh * (group // 2) + pair) * LANES, (kvh * (group // 2) + pair + 1) * LANES)
                o_pair = jnp.where(lo_half, o4[(2 * pair) * w_:(2 * pair + 1) * w_],
                                   o4[(2 * pair + 1) * w_:(2 * pair + 2) * w_])
                osw_ref[0, blk * w_:(blk + 1) * w_, cols] = o_pair.astype(osw_ref.dtype)

    lax.fori_loop(0, n_chunks // WY_GROUP, wy_group_body, 0)
    lax.fori_loop(0, n_chunks, chunk_body, 0)
    deltanet_outputs()
    swa_outputs()
    kvbuf[0:w_, :] = kvbuf[tm:tm + w_, :]


def _mixers(h, w_mix, layer, convw, alog_row, dtb_row, normw_row, sinks):
    b, s, d = h.shape
    tm = min(TOKEN_TILE, s)
    n_qkv = 3 * DN_HEADS * DN_DK
    dn_w = DN_HEADS * DN_DV
    sw_w = SWA_HEADS * SWA_DH
    n_slots = (tm // DN_CHUNK) * DN_HEADS
    assert s % tm == 0 and tm % WINDOW == 0 and tm % (WY_GROUP * DN_CHUNK) == 0, (s, tm)
    assert w_mix.shape[-1] == MIX_WIDTH and d % LANES == 0, (w_mix.shape, d)
    full = lambda shape: pl.BlockSpec(shape, lambda bi, ji: tuple(0 for _ in shape))
    tok = lambda width: pl.BlockSpec((1, tm, width), lambda bi, ji: (bi, ji, 0))
    return pl.pallas_call(
        functools.partial(_mixers_kernel, tm=tm),
        out_shape=(jax.ShapeDtypeStruct((b, s, dn_w), BF16), jax.ShapeDtypeStruct((b, s, sw_w), BF16)),
        grid=(b, s // tm),
        in_specs=[
            tok(d),
            pl.BlockSpec((None, d, w_mix.shape[-1]), lambda bi, ji: (layer, 0, 0)),
            full(convw.shape), full(alog_row.shape), full(dtb_row.shape), full(normw_row.shape),
            pl.BlockSpec(memory_space=pltpu.SMEM),
        ],
        out_specs=(tok(dn_w), tok(sw_w)),
        scratch_shapes=[
            pltpu.VMEM((tm + SUBLANES, n_qkv), F32),
            pltpu.VMEM((tm, dn_w), BF16),
            pltpu.VMEM((tm, dn_w), BF16),
            pltpu.VMEM((tm, dn_w), BF16),
            pltpu.VMEM((tm, dn_w), BF16),
            pltpu.VMEM((tm, dn_w), BF16),
            pltpu.VMEM((tm, 2 * dn_w), BF16),
            pltpu.VMEM((tm, dn_w), F32),
            pltpu.VMEM((tm, LANES), F32),
            pltpu.VMEM((tm, DN_HEADS * DN_CHUNK), F32),
            pltpu.VMEM((tm, LANES), F32),
            pltpu.VMEM((tm, dn_w), F32),
            pltpu.VMEM((n_slots, DN_DK + DN_CHUNK, DN_DK), BF16),
            pltpu.VMEM((n_slots, DN_DK, DN_DV), F32),
            pltpu.VMEM((DN_HEADS, DN_DK, DN_DV), F32),
            pltpu.VMEM((tm + WINDOW, 4 * LANES), BF16),
            pltpu.VMEM((tm, sw_w), BF16),
            pltpu.VMEM((tm, sw_w), BF16),
        ],
        compiler_params=pltpu.CompilerParams(
            dimension_semantics=("arbitrary", "arbitrary"), vmem_limit_bytes=VMEM_LIMIT_BYTES),
        name="mixers",
    )(h, w_mix, convw, alog_row, dtb_row, normw_row, sinks)


def _merge_kernel(h_ref, odn_ref, osw_ref, mem_ref, memg_ref, memb_ref, wmkv_ref, w_ref,
                  wb_ref, wo_ref, g_ref, b_ref, wgu_ref, wd_ref, g2_ref, b2_ref, o_ref,
                  kmem, vmem, merged, acc_ref, *, alpha):
    j = pl.program_id(1)
    xa_w = XA_HEADS * XA_DH

    @pl.when(j == 0)
    def _memory_kv():
        mem_n = _layer_norm(mem_ref[0], memg_ref[...], memb_ref[...])
        kvm = _dot(mem_n.astype(BF16), wmkv_ref[...])
        kmem[...] = kvm[:, :xa_w].astype(BF16)
        vmem[...] = kvm[:, xa_w:].astype(BF16)

    hf = h_ref[0]
    hb = hf.astype(BF16)
    xq = _dot(hb, w_ref[:, MRG_XAQ]) * (XA_DH ** -0.5)
    hsl = [slice(hd * XA_DH, (hd + 1) * XA_DH) for hd in range(XA_HEADS)]
    xqb = xq.astype(BF16)
    scores = [_dot_nt(xqb[:, sl], kmem[:, sl]) for sl in hsl]
    probs = [jnp.exp(s - jnp.max(s, axis=-1, keepdims=True)) for s in scores]
    denoms = [jnp.sum(p, axis=-1, keepdims=True) for p in probs]
    heads = [_dot(p.astype(BF16), vmem[:, sl]) / dn for p, dn, sl in zip(probs, denoms, hsl)]
    oxa = jnp.concatenate(heads, axis=1).astype(BF16)
    branches = (odn_ref[0], osw_ref[0], oxa)
    d = hf.shape[-1]
    for dc in range(d // MERGE_CHUNK):
        lo = dc * MERGE_CHUNK
        acc = None
        for n in range(N_BRANCH):
            g0 = MRG_GATE0 + n * d + lo
            gate = _sigmoid(_dot(hb, w_ref[:, g0:g0 + MERGE_CHUNK]))
            term = gate * _dot(branches[n], wb_ref[n, :, lo:lo + MERGE_CHUNK])
            acc = term if acc is None else acc + term
        merged[:, lo:lo + MERGE_CHUNK] = acc.astype(BF16)
    y = alpha * hf + _dot(merged[...], wo_ref[...])
    h2 = _layer_norm(y, g_ref[...], b_ref[...])
    d_ff = wd_ref.shape[0]
    h2b = h2.astype(BF16)
    for c in range(d_ff // FF_CHUNK):
        lo = c * FF_CHUNK
        gate = _dot(h2b, wgu_ref[:, lo:lo + FF_CHUNK])
        up = _dot(h2b, wgu_ref[:, d_ff + lo:d_ff + lo + FF_CHUNK])
        part = _dot((_silu(gate) * up).astype(BF16), wd_ref[lo:lo + FF_CHUNK, :])
        if c == 0:
            acc_ref[...] = part
        else:
            acc_ref[...] += part
    o_ref[0] = _layer_norm(alpha * h2 + acc_ref[...], g2_ref[...], b2_ref[...])


def _merge(h, odn, osw, mem, memg, memb, wmkv, w_mrg, wbranch, wout, layer, g, b_, w_gu, w_down, g2, b2,
           alpha):
    b, s, d = h.shape
    tm = min(MERGE_TOKEN_TILE, s)
    n_mem = mem.shape[1]
    xa_w = XA_HEADS * XA_DH
    assert s % tm == 0 and d % MERGE_CHUNK == 0 and w_mrg.shape[-1] == MRG_GATE0 + N_BRANCH * d, (s, d)
    full = lambda shape: pl.BlockSpec(shape, lambda bi, ji: tuple(0 for _ in shape))
    tok = lambda width: pl.BlockSpec((1, tm, width), lambda bi, ji: (bi, ji, 0))
    layer_block = lambda w: pl.BlockSpec((None,) + w.shape[1:],
                                         lambda bi, ji: (layer,) + (0,) * (w.ndim - 1),
                                         pipeline_mode=pl.Buffered(1))
    return pl.pallas_call(
        functools.partial(_merge_kernel, alpha=alpha),
        out_shape=jax.ShapeDtypeStruct((b, s, d), F32),
        grid=(b, s // tm),
        in_specs=[
            tok(d), tok(odn.shape[-1]), tok(osw.shape[-1]),
            pl.BlockSpec((1, n_mem, d), lambda bi, ji: (bi, 0, 0)),
            full(memg.shape), full(memb.shape), layer_block(wmkv), layer_block(w_mrg),
            layer_block(wbranch), layer_block(wout), full(g.shape), full(b_.shape),
            layer_block(w_gu), layer_block(w_down), full(g2.shape), full(b2.shape),
        ],
        out_specs=tok(d),
        scratch_shapes=[
            pltpu.VMEM((n_mem, xa_w), BF16),
            pltpu.VMEM((n_mem, xa_w), BF16),
            pltpu.VMEM((tm, d), BF16),
            pltpu.VMEM((tm, d), F32),
        ],
        compiler_params=pltpu.CompilerParams(
            dimension_semantics=("arbitrary", "arbitrary"), vmem_limit_bytes=VMEM_LIMIT_BYTES),
        name="merge",
    )(h, odn, osw, mem, memg, memb, wmkv, w_mrg, wbranch, wout, g, b_, w_gu, w_down, g2, b2)


def _lane_row(vals, offset):
    return jnp.zeros((1, LANES), F32).at[0, offset:offset + vals.shape[0]].set(vals.astype(F32))


def _pack_kernel(main_ref, tail_ref, mix_ref, mrg_ref):
    o_b = MIX_QKV.stop
    shift = 2 * DN_HEADS
    n_mid = MIX_SWKV.stop - MIX_Z.start
    n_mrg = mrg_ref.shape[-1]
    mrg_lo = o_b + n_mid
    lane = lax.broadcasted_iota(jnp.int32, (main_ref.shape[0], LANES), 1)

    mix_ref[:, MIX_QKV] = main_ref[:, MIX_QKV].astype(BF16)
    mid_w = -(-(shift + n_mid) // LANES) * LANES
    mid = pltpu.roll(main_ref[:, o_b:o_b + mid_w], mid_w - shift, axis=1)
    mix_ref[:, MIX_Z.start:MIX_SWKV.stop] = mid[:, :n_mid].astype(BF16)
    first = main_ref[:, o_b:o_b + LANES]
    gate_lanes = (lane >= DN_HEADS) & (lane < 2 * DN_HEADS)
    mix_ref[:, MIX_BA.start:MIX_BA.start + LANES] = jnp.where(
        gate_lanes, pltpu.roll(first, DN_HEADS, axis=1), 0.0).astype(BF16)
    mix_ref[:, MIX_BA.start + LANES:MIX_BA.stop] = jnp.where(gate_lanes, first, 0.0).astype(BF16)

    big = pltpu.roll(main_ref[:, mrg_lo:mrg_lo + n_mrg], n_mrg - shift, axis=1)
    mrg_ref[:, :n_mrg - LANES] = big[:, :n_mrg - LANES].astype(BF16)
    tail = pltpu.roll(tail_ref[...], LANES - shift, axis=1)
    mrg_ref[:, n_mrg - LANES:] = jnp.where(lane < LANES - shift, big[:, n_mrg - LANES:], tail).astype(BF16)


def _pack_input_projection(w_in):
    depth, d, d_in = w_in.shape
    n_mrg = d_in - (MIX_QKV.stop + 2 * DN_HEADS + MIX_SWKV.stop - MIX_Z.start)
    main_w = (d_in // LANES) * LANES
    assert d_in - main_w == 2 * DN_HEADS and n_mrg % LANES == 0
    rows = PACK_ROWS
    return pl.pallas_call(
        _pack_kernel,
        out_shape=(jax.ShapeDtypeStruct((depth, d, MIX_WIDTH), BF16),
                   jax.ShapeDtypeStruct((depth, d, n_mrg), BF16)),
        grid=(depth, d // rows),
        in_specs=[pl.BlockSpec((None, rows, main_w), lambda l, i: (l, i, 0)),
                  pl.BlockSpec((None, rows, LANES), lambda l, i: (l, i, main_w // LANES))],
        out_specs=(pl.BlockSpec((None, rows, MIX_WIDTH), lambda l, i: (l, i, 0)),
                   pl.BlockSpec((None, rows, n_mrg), lambda l, i: (l, i, 0))),
        compiler_params=pltpu.CompilerParams(
            dimension_semantics=("arbitrary", "arbitrary"), vmem_limit_bytes=VMEM_LIMIT_BYTES),
        name="pack_w_in",
    )(w_in, w_in)


def _mixer_tables(conv_w, a_log, dt_bias, norm_w, sinks):
    return (conv_w.astype(F32), _lane_row(a_log, DN_HEADS), _lane_row(dt_bias, DN_HEADS),
            norm_w.reshape(1, -1).astype(F32), sinks.astype(F32))


def kernel(x, mem, mem_ln_g, mem_ln_b, ln_g, ln_b, ffn1_w_gu, ffn1_w_down, w_in, dn_conv_w, dn_a_log,
           dn_dt_bias, dn_norm_w, swa_sinks, w_mem_kv, w_branch, w_out, ffn2_w_gu, ffn2_w_down):
    b, s, d = x.shape
    depth = ln_g.shape[0]
    alpha = float((2 * depth) ** 0.25)
    row = lambda v: v.reshape(1, -1).astype(F32)

    w_mix, w_mrg = _pack_input_projection(w_in)
    ffn1_gu, ffn1_down = ffn1_w_gu.astype(BF16), (MACARON_SCALE * ffn1_w_down).astype(BF16)
    ffn2_gu, ffn2_down = ffn2_w_gu.astype(BF16), (MACARON_SCALE * ffn2_w_down).astype(BF16)
    w_mem_kv_b, w_branch_b, w_out_b = w_mem_kv.astype(BF16), w_branch.astype(BF16), w_out.astype(BF16)

    for l in range(depth):
        h = _ffn_ln(x.reshape(b * s, d), ffn1_gu, ffn1_down, l, row(ln_g[l, 0]), row(ln_b[l, 0]), alpha)
        h = h.reshape(b, s, d)
        odn, osw = _mixers(h, w_mix, l, *_mixer_tables(dn_conv_w[l], dn_a_log[l], dn_dt_bias[l],
                                                       dn_norm_w[l], swa_sinks[l]))
        x = _merge(h, odn, osw, mem, row(mem_ln_g), row(mem_ln_b), w_mem_kv_b, w_mrg, w_branch_b,
                   w_out_b, l, row(ln_g[l, 1]), row(ln_b[l, 1]), ffn2_gu, ffn2_down,
                   row(ln_g[l, 2]), row(ln_b[l, 2]), alpha)
    return x
```
